```python
import math
import jax
import jax.numpy as jnp
from jax import lax
import numpy as np

D_MODEL = 1024
BATCH = 16
SEQ = 2048
DEPTH = 2

N_MIXERS = 2
ATTN_PATTERNS = ((128, 1), (512, 4), (2048, 16))
N_ATTN_GROUPS = len(ATTN_PATTERNS)
HEADS_PER_GROUP = 8
HEAD_DIM = 64
ATTN_WIDTH = HEADS_PER_GROUP * HEAD_DIM
ROT_DIM = HEAD_DIM // 4
ROPE_THETA = 500000.0
ATTN_BLOCK = 128
POOL_WINDOWS = (2, 4, 8, 16)
N_POOL_GROUPS = len(POOL_WINDOWS)
POOL_GROUP_DIM = D_MODEL // N_POOL_GROUPS
N_EXPERT_GROUPS = 4
EXPERTS_PER_GROUP = 4
N_EXPERTS = N_EXPERT_GROUPS * EXPERTS_PER_GROUP
EXPERT_TOP_K = 2
D_EXPERT = D_MODEL // 2
MOE_BLOCK = 256
RMS_EPS = 1e-6

kernel_name = "hybrid_dilated_attn_pool_hmoe"


def rmsnorm(x, g):
    xf = x.astype(jnp.float32)
    y = xf * lax.rsqrt(jnp.mean(xf * xf, axis=-1, keepdims=True) + RMS_EPS)
    return (y * g.astype(jnp.float32)).astype(x.dtype)


def rotary_tables(positions):
    inv_freq = ROPE_THETA ** (-(jnp.arange(ROT_DIM // 2, dtype=jnp.float32) * 2.0 / ROT_DIM))
    ang = positions.astype(jnp.float32)[..., None] * inv_freq
    return jnp.cos(ang)[:, :, None, :], jnp.sin(ang)[:, :, None, :]


def partial_rotary(t, cos, sin):
    half = ROT_DIM // 2
    cos = cos.astype(t.dtype)
    sin = sin.astype(t.dtype)
    t1, t2, rest = t[..., :half], t[..., half:ROT_DIM], t[..., ROT_DIM:]
    return jnp.concatenate([t1 * cos - t2 * sin, t2 * cos + t1 * sin, rest], axis=-1)


def dilated_window_attention(q, k, v, dil, steps):
    B, S, H, Dh = q.shape
    L = S // dil
    nb = -(-L // ATTN_BLOCK)
    Lp = nb * ATTN_BLOCK

    def to_phase(t):
        t = t.reshape(B, L, dil, H, Dh).transpose(0, 2, 3, 1, 4).reshape(B * dil, H, L, Dh)
        t = jnp.pad(t, ((0, 0), (0, 0), (0, Lp - L), (0, 0)))
        return t.reshape(B * dil, H, nb, ATTN_BLOCK, Dh)

    def with_prev(t):
        prev = jnp.pad(t[:, :, :-1], ((0, 0), (0, 0), (1, 0), (0, 0), (0, 0)))
        return jnp.concatenate([prev, t], axis=3)

    qb = to_phase(q)
    kw = with_prev(to_phase(k))
    vw = with_prev(to_phase(v))
    s = jnp.einsum('nhbqd,nhbkd->nhbqk', qb, kw).astype(jnp.float32) * (1.0 / math.sqrt(Dh))
    blk = jnp.arange(nb)[:, None, None] * ATTN_BLOCK
    qpos = blk + jnp.arange(ATTN_BLOCK)[None, :, None]
    kpos = blk - ATTN_BLOCK + jnp.arange(2 * ATTN_BLOCK)[None, None, :]
    dist = qpos - kpos
    valid = (dist >= 0) & (dist <= steps) & (kpos >= 0)
    s = jnp.where(valid, s, -jnp.inf)
    lse = jax.nn.logsumexp(s, axis=-1)
    p = jnp.exp(s - lse[..., None]).astype(v.dtype)
    o = jnp.einsum('nhbqk,nhbkd->nhbqd', p, vw)
    o = o.reshape(B, dil, H, Lp, Dh)[:, :, :, :L].transpose(0, 3, 1, 2, 4).reshape(B, S, H, Dh)
    lse = lse.reshape(B, dil, H, Lp)[:, :, :, :L].transpose(0, 3, 1, 2).reshape(B, S, H)
    return o, lse


def dilated_attention_mixer(y, w_in, w_out, cos, sin):
    B, S, _ = y.shape
    qkv = (y @ w_in).reshape(B, S, N_ATTN_GROUPS, 3, HEADS_PER_GROUP, HEAD_DIM)
    outs, lses = [], []
    for g, (window, dil) in enumerate(ATTN_PATTERNS):
        q = partial_rotary(qkv[:, :, g, 0], cos, sin)
        k = partial_rotary(qkv[:, :, g, 1], cos, sin)
        o, lse = dilated_window_attention(q, k, qkv[:, :, g, 2], dil, window // dil)
        outs.append(o)
        lses.append(lse)
    wts = jax.nn.softmax(jnp.stack(lses, axis=0), axis=0)
    o = jnp.sum(wts[..., None].astype(y.dtype) * jnp.stack(outs, axis=0), axis=0)
    return o.reshape(B, S, ATTN_WIDTH) @ w_out


def multiscale_pool_mixer(y, w_in, w_group, scale, w_out):
    B, S, D = y.shape
    u = (y @ w_in).reshape(B, S, N_POOL_GROUPS, POOL_GROUP_DIM)
    c = jnp.cumsum(u.astype(jnp.float32), axis=1)
    t1 = jnp.arange(S, dtype=jnp.int32) + 1
    outs = []
    for g, w in enumerate(POOL_WINDOWS):
        cg = c[:, :, g]
        shifted = jnp.pad(cg[:, :S - w], ((0, 0), (w, 0), (0, 0)))
        count = jnp.minimum(t1, w).astype(jnp.float32)[None, :, None]
        outs.append((cg - shifted) / count - u[:, :, g].astype(jnp.float32))
    pooled = jnp.stack(outs, axis=2).astype(y.dtype)
    z = jnp.einsum('bsgc,gce->bsge', pooled, w_group).reshape(B, S, D) * scale
    return z @ w_out


def routed_experts(xf, expert_idx, gates, w1, w3, w2):
    T, Dm = xf.shape
    K = expert_idx.shape[1]
    E = w1.shape[0]
    N = T * K
    flat_e = expert_idx.reshape(N)
    flat_tok = jnp.repeat(jnp.arange(T, dtype=jnp.int32), K)
    flat_g = gates.reshape(N).astype(xf.dtype)
    order = jnp.argsort(flat_e)
    se = flat_e[order]
    counts = jnp.bincount(flat_e, length=E)
    starts = jnp.cumsum(counts) - counts
    padded = ((counts + MOE_BLOCK - 1) // MOE_BLOCK) * MOE_BLOCK
    pends = jnp.cumsum(padded)
    pstarts = pends - padded
    dest = pstarts[se] + (jnp.arange(N, dtype=jnp.int32) - starts[se])
    n_blocks = -(-(N + E * (MOE_BLOCK - 1)) // MOE_BLOCK)
    P = n_blocks * MOE_BLOCK
    row_tok = jnp.full((P,), T, dtype=jnp.int32).at[dest].set(flat_tok[order])
    row_gate = jnp.zeros((P,), xf.dtype).at[dest].set(flat_g[order])
    block_e = jnp.minimum(
        jnp.searchsorted(pends, jnp.arange(n_blocks, dtype=jnp.int32) * MOE_BLOCK, side='right'), E - 1)
    xs = jnp.concatenate([xf, jnp.zeros((1, Dm), xf.dtype)], axis=0)[row_tok]
    xs = xs.reshape(n_blocks, MOE_BLOCK, Dm)

    def expert_block(args):
        xb, e = args
        hmid = jax.nn.silu(xb @ w1[e]) * (xb @ w3[e])
        return hmid @ w2[e]

    ys = lax.map(expert_block, (xs, block_e)).reshape(P, Dm) * row_gate[:, None]
    return jnp.zeros((T + 1, Dm), ys.dtype).at[row_tok].add(ys)[:T]


def hierarchical_moe(y, wg, bg, we, be, w1, w3, w2):
    B, S, D = y.shape
    xf = y.reshape(B * S, D)
    lg = (xf @ wg).astype(jnp.float32) + bg.astype(jnp.float32)
    pg = jax.nn.softmax(lg, axis=-1)
    g_star = jnp.argmax(pg, axis=-1).astype(jnp.int32)
    gate1 = jnp.take_along_axis(pg, g_star[:, None], axis=1)[:, 0]
    le = jnp.einsum('td,gde->tge', xf, we).astype(jnp.float32) + be.astype(jnp.float32)
    le = jnp.take_along_axis(le, g_star[:, None, None], axis=1)[:, 0]
    top_vals, top_idx = lax.top_k(le, EXPERT_TOP_K)
    gates = gate1[:, None] * jax.nn.softmax(top_vals, axis=-1)
    expert_idx = g_star[:, None] * EXPERTS_PER_GROUP + top_idx.astype(jnp.int32)
    out = routed_experts(xf, expert_idx, gates, w1, w3, w2)
    return out.reshape(B, S, D)


def setup_inputs(seed: int = 0) -> dict:
    key = jax.random.key(seed)
    ks = jax.random.split(key, 20)
    n_a = (DEPTH + 1) // 2
    n_b = DEPTH // 2
    D = D_MODEL
    nrm = jax.random.normal
    x = nrm(ks[0], (BATCH, SEQ, D), jnp.float32)
    offs = jax.random.randint(ks[1], (BATCH, 1), 0, 4096, dtype=jnp.int32)
    positions = (offs + jnp.arange(SEQ, dtype=jnp.int32)[None, :]).astype(jnp.int32)
    return {
        "x": x,
        "positions": positions,
        "norm_mix": 1.0 + 0.05 * nrm(ks[2], (DEPTH, D), jnp.float32),
        "norm_ffn": 1.0 + 0.05 * nrm(ks[3], (DEPTH, D), jnp.float32),
        "norm_final": 1.0 + 0.05 * nrm(ks[4], (D,), jnp.float32),
        "attn_w_in": nrm(ks[5], (n_a, D, N_ATTN_GROUPS * 3 * ATTN_WIDTH), jnp.float32) * D ** -0.5,
        "attn_w_out": nrm(ks[6], (n_a, ATTN_WIDTH, D), jnp.float32) * ATTN_WIDTH ** -0.5,
        "pool_w_in": nrm(ks[7], (n_b, D, D), jnp.float32) * D ** -0.5,
        "pool_w_group": nrm(ks[8], (n_b, N_POOL_GROUPS, POOL_GROUP_DIM, POOL_GROUP_DIM), jnp.float32) * POOL_GROUP_DIM ** -0.5,
        "pool_scale": 1.0 + 0.1 * nrm(ks[9], (n_b, D), jnp.float32),
        "pool_w_out": nrm(ks[10], (n_b, D, D), jnp.float32) * D ** -0.5,
        "router_group_w": nrm(ks[11], (DEPTH, D, N_EXPERT_GROUPS), jnp.float32) * D ** -0.5,
        "router_group_b": 0.01 * nrm(ks[12], (DEPTH, N_EXPERT_GROUPS), jnp.float32),
        "router_expert_w": nrm(ks[13], (DEPTH, N_EXPERT_GROUPS, D, EXPERTS_PER_GROUP), jnp.float32) * D ** -0.5,
        "router_expert_b": 0.01 * nrm(ks[14], (DEPTH, N_EXPERT_GROUPS, EXPERTS_PER_GROUP), jnp.float32),
        "expert_w1": nrm(ks[15], (DEPTH, N_EXPERTS, D, D_EXPERT), jnp.float32) * D ** -0.5,
        "expert_w3": nrm(ks[16], (DEPTH, N_EXPERTS, D, D_EXPERT), jnp.float32) * D ** -0.5,
        "expert_w2": nrm(ks[17], (DEPTH, N_EXPERTS, D_EXPERT, D), jnp.float32) * D_EXPERT ** -0.5,
    }


def reference(x, positions, norm_mix, norm_ffn, norm_final, attn_w_in, attn_w_out,
              pool_w_in, pool_w_group, pool_scale, pool_w_out,
              router_group_w, router_group_b, router_expert_w, router_expert_b,
              expert_w1, expert_w3, expert_w2):
    cos, sin = rotary_tables(positions)
    h = x
    for i in range(DEPTH):
        y = rmsnorm(h, norm_mix[i])
        j = i // N_MIXERS
        if i % N_MIXERS == 0:
            h = h + dilated_attention_mixer(y, attn_w_in[j], attn_w_out[j], cos, sin)
        else:
            h = h + multiscale_pool_mixer(y, pool_w_in[j], pool_w_group[j], pool_scale[j], pool_w_out[j])
        y = rmsnorm(h, norm_ffn[i])
        h = h + hierarchical_moe(y, router_group_w[i], router_group_b[i],
                                 router_expert_w[i], router_expert_b[i],
                                 expert_w1[i], expert_w3[i], expert_w2[i])
    return rmsnorm(h, norm_final)
```

```python
import functools
import math

import jax
import jax.numpy as jnp
from jax import lax
from jax.experimental import pallas as pl
from jax.experimental.pallas import tpu as pltpu

F32 = jnp.float32
BF16 = jnp.bfloat16

D_MODEL = 1024
ATTN_PATTERNS = ((128, 1), (512, 4), (2048, 16))
N_ATTN_GROUPS = len(ATTN_PATTERNS)
HEADS = 8
HEAD_DIM = 64
ATTN_WIDTH = HEADS * HEAD_DIM
QKV_WIDTH = N_ATTN_GROUPS * 3 * ATTN_WIDTH
ROT_DIM = HEAD_DIM // 4
ROPE_THETA = 500000.0
ATTN_BLOCK = 128
POOL_WINDOWS = (2, 4, 8, 16)
N_POOL_GROUPS = len(POOL_WINDOWS)
POOL_GROUP_DIM = D_MODEL // N_POOL_GROUPS
N_EXPERT_GROUPS = 4
EXPERTS_PER_GROUP = 4
N_EXPERTS = N_EXPERT_GROUPS * EXPERTS_PER_GROUP
D_EXPERT = D_MODEL // 2
RMS_EPS = 1e-6

PAIRS = ((0, 1), (0, 2), (0, 3), (1, 2), (1, 3), (2, 3))
N_CLASSES = N_EXPERT_GROUPS * len(PAIRS)

ROW_TILE = 512
COL_TILE = 512
MOE_ROWS = 256
LANES = 128
VMEM_LIMIT = 48 * 1024 * 1024
MASKED = -1e30


def _params(sem):
    return pltpu.CompilerParams(dimension_semantics=sem, vmem_limit_bytes=VMEM_LIMIT)


def _rmsnorm_rows(x, g):
    ms = jnp.mean(x * x, axis=-1, keepdims=True)
    return (x * lax.rsqrt(ms + RMS_EPS)) * g


def _rope_kernel(pos_ref, freq_ref, c_ref, s1_ref, s2_ref):
    ang = pos_ref[...].astype(F32) * freq_ref[...]
    lane = lax.broadcasted_iota(jnp.int32, ang.shape, 1) % HEAD_DIM
    sin = jnp.sin(ang)
    c_ref[...] = jnp.cos(ang)
    s1_ref[...] = jnp.where(lane < ROT_DIM // 2, -sin, 0.0)
    s2_ref[...] = jnp.where(lane >= ROT_DIM // 2, sin, 0.0)


def _rope_tables(positions):
    t = positions.size
    half = ROT_DIM // 2
    inv = [ROPE_THETA ** (-(i * 2.0 / ROT_DIM)) for i in range(half)]
    per_head = inv + inv + [0.0] * (HEAD_DIM - ROT_DIM)
    freq = jnp.asarray([per_head * (LANES // HEAD_DIM)], F32)
    tm = 1024
    out = jax.ShapeDtypeStruct((t, LANES), F32)
    spec = pl.BlockSpec((tm, LANES), lambda i: (i, 0))
    return pl.pallas_call(
        _rope_kernel,
        grid=(t // tm,),
        in_specs=[pl.BlockSpec((tm, 1), lambda i: (i, 0)),
                  pl.BlockSpec((1, LANES), lambda i: (0, 0))],
        out_specs=[spec, spec, spec],
        out_shape=[out, out, out],
        compiler_params=_params(("parallel",)),
        name="rope_tables",
    )(positions.reshape(t, 1), freq)


def _qkv_kernel(x_ref, g_ref, w_ref, c_ref, s1_ref, s2_ref, o_ref, y_sc):
    j = pl.program_id(1)

    @pl.when(j == 0)
    def _():
        y_sc[...] = _rmsnorm_rows(x_ref[...], g_ref[...]).astype(BF16)

    acc = jnp.dot(y_sc[...], w_ref[...], preferred_element_type=F32)

    @pl.when(lax.rem(j, 3) != 2)
    def _():
        reps = COL_TILE // LANES
        c = jnp.concatenate([c_ref[...]] * reps, axis=1)
        s1 = jnp.concatenate([s1_ref[...]] * reps, axis=1)
        s2 = jnp.concatenate([s2_ref[...]] * reps, axis=1)
        up = pltpu.roll(acc, COL_TILE - ROT_DIM // 2, 1)
        dn = pltpu.roll(acc, ROT_DIM // 2, 1)
        o_ref[...] = (acc * c + up * s1 + dn * s2).astype(BF16)

    @pl.when(lax.rem(j, 3) == 2)
    def _():
        o_ref[...] = acc.astype(BF16)


def _qkv_proj(x, gain, w, c, s1, s2):
    t = x.shape[0]
    tab = pl.BlockSpec((ROW_TILE, LANES), lambda i, j: (i, 0))
    return pl.pallas_call(
        _qkv_kernel,
        grid=(t // ROW_TILE, QKV_WIDTH // COL_TILE),
        in_specs=[pl.BlockSpec((ROW_TILE, D_MODEL), lambda i, j: (i, 0)),
                  pl.BlockSpec((1, D_MODEL), lambda i, j: (0, 0)),
                  pl.BlockSpec((D_MODEL, COL_TILE), lambda i, j: (0, j)),
                  tab, tab, tab],
        out_specs=pl.BlockSpec((ROW_TILE, COL_TILE), lambda i, j: (i, j)),
        out_shape=jax.ShapeDtypeStruct((t, QKV_WIDTH), BF16),
        scratch_shapes=[pltpu.VMEM((ROW_TILE, D_MODEL), BF16)],
        compiler_params=_params(("parallel", "arbitrary")),
        name="qkv_proj",
    )(x, gain.reshape(1, D_MODEL), w, c, s1, s2)


def _attn_kernel(q_ref, kp_ref, kc_ref, vp_ref, vc_ref, o_ref, l_ref):
    j = pl.program_id(2)
    blk = ATTN_BLOCK
    row = lax.broadcasted_iota(jnp.int32, (blk, 2 * blk), 0)
    col = lax.broadcasted_iota(jnp.int32, (blk, 2 * blk), 1)
    lo = jnp.maximum(row, jnp.where(j == 0, blk, 0))
    bias = jnp.where(col >= lo, jnp.where(col <= row + blk, 0.0, MASKED), MASKED)
    scale = 1.0 / math.sqrt(HEAD_DIM)
    outs, lses = [], []
    for h in range(HEADS):
        sl = slice(h * HEAD_DIM, (h + 1) * HEAD_DIM)
        q = q_ref[:, sl]
        k = jnp.concatenate([kp_ref[:, sl], kc_ref[:, sl]], axis=0)
        v = jnp.concatenate([vp_ref[:, sl], vc_ref[:, sl]], axis=0)
        s = lax.dot_general(q, k, (((1,), (1,)), ((), ())), preferred_element_type=F32)
        s = s * scale + bias
        m = jnp.max(s, axis=-1, keepdims=True)
        p = jnp.exp(s - m)
        den = jnp.sum(p, axis=-1, keepdims=True)
        o = jnp.dot(p.astype(BF16), v, preferred_element_type=F32) / den
        outs.append(o)
        lses.append(jnp.broadcast_to(m + jnp.log(den), (blk, HEAD_DIM)))
    o_ref[...] = jnp.concatenate(outs, axis=1).astype(BF16)
    l_ref[...] = jnp.concatenate(lses, axis=1)


def _dilated_attention(qkv, group, batch, seq):
    _, dil = ATTN_PATTERNS[group]
    length = seq // dil
    nb = length // ATTN_BLOCK
    cols = QKV_WIDTH // ATTN_WIDTH
    view = qkv.reshape(batch, length, dil * QKV_WIDTH)

    def spec(kind, prev):
        def index(b, p, j):
            jj = jnp.maximum(j - 1, 0) if prev else j
            return (b, jj, p * cols + group * 3 + kind)
        return pl.BlockSpec((None, ATTN_BLOCK, ATTN_WIDTH), index)

    out_spec = pl.BlockSpec((None, ATTN_BLOCK, ATTN_WIDTH), lambda b, p, j: (b, j, p))
    o, lse = pl.pallas_call(
        _attn_kernel,
        grid=(batch, dil, nb),
        in_specs=[spec(0, False), spec(1, True), spec(1, False), spec(2, True), spec(2, False)],
        out_specs=[out_spec, out_spec],
        out_shape=[jax.ShapeDtypeStruct((batch, length, dil * ATTN_WIDTH), BF16),
                   jax.ShapeDtypeStruct((batch, length, dil * ATTN_WIDTH), F32)],
        compiler_params=_params(("parallel", "parallel", "arbitrary")),
        name=f"dilated_attn_g{group}",
    )(view, view, view, view, view)
    return o.reshape(batch * seq, ATTN_WIDTH), lse.reshape(batch * seq, ATTN_WIDTH)


def _merge_kernel(o0, o1, o2, l0, l1, l2, w_ref, h_ref, out_ref):
    a, b, c = l0[...], l1[...], l2[...]
    m = jnp.maximum(jnp.maximum(a, b), c)
    ea, eb, ec = jnp.exp(a - m), jnp.exp(b - m), jnp.exp(c - m)
    mix = (ea * o0[...].astype(F32) + eb * o1[...].astype(F32) + ec * o2[...].astype(F32)) / (ea + eb + ec)
    out_ref[...] = h_ref[...] + jnp.dot(mix.astype(BF16), w_ref[...], preferred_element_type=F32)


def _merge_out_proj(outs, lses, w, h):
    t = h.shape[0]
    half = pl.BlockSpec((ROW_TILE, ATTN_WIDTH), lambda i: (i, 0))
    full = pl.BlockSpec((ROW_TILE, D_MODEL), lambda i: (i, 0))
    return pl.pallas_call(
        _merge_kernel,
        grid=(t // ROW_TILE,),
        in_specs=[half] * 6 + [pl.BlockSpec((ATTN_WIDTH, D_MODEL), lambda i: (0, 0)), full],
        out_specs=full,
        out_shape=jax.ShapeDtypeStruct((t, D_MODEL), F32),
        compiler_params=_params(("parallel",)),
        name="attn_merge_out",
    )(*outs, *lses, w, h)


def _router_kernel(x_ref, g_ref, whi_ref, wlo_ref, y_ref, lg_ref):
    y = _rmsnorm_rows(x_ref[...], g_ref[...])
    y_hi = y.astype(BF16)
    y_lo = (y - y_hi.astype(F32)).astype(BF16)
    whi = whi_ref[...]
    lg = jnp.dot(y_hi, whi, preferred_element_type=F32)
    lg += jnp.dot(y_lo, whi, preferred_element_type=F32)
    lg += jnp.dot(y_hi, wlo_ref[...], preferred_element_type=F32)
    y_ref[...] = y_hi
    lg_ref[...] = lg


def _router_logits(h, gain, wcat):
    t = h.shape[0]
    whi = wcat.astype(BF16)
    wlo = (wcat - whi.astype(F32)).astype(BF16)
    wspec = pl.BlockSpec((D_MODEL, LANES), lambda i: (0, 0))
    return pl.pallas_call(
        _router_kernel,
        grid=(t // ROW_TILE,),
        in_specs=[pl.BlockSpec((ROW_TILE, D_MODEL), lambda i: (i, 0)),
                  pl.BlockSpec((1, D_MODEL), lambda i: (0, 0)), wspec, wspec],
        out_specs=[pl.BlockSpec((ROW_TILE, D_MODEL), lambda i: (i, 0)),
                   pl.BlockSpec((ROW_TILE, LANES), lambda i: (i, 0))],
        out_shape=[jax.ShapeDtypeStruct((t, D_MODEL), BF16),
                   jax.ShapeDtypeStruct((t, LANES), F32)],
        compiler_params=_params(("parallel",)),
        name="router_logits",
    )(h, gain.reshape(1, D_MODEL), whi, wlo)


def _silu(a):
    return a / (1.0 + jnp.exp(-a))


def _moe_kernel(ea_ref, eb_ref, x_ref, g_ref, w1a, w3a, w2a, w1b, w3b, w2b, o_ref):
    del ea_ref, eb_ref
    x = x_ref[...]
    g = g_ref[...]
    ha = _silu(jnp.dot(x, w1a[...], preferred_element_type=F32))
    ha = ha * jnp.dot(x, w3a[...], preferred_element_type=F32) * g[:, 0:1]
    hb = _silu(jnp.dot(x, w1b[...], preferred_element_type=F32))
    hb = hb * jnp.dot(x, w3b[...], preferred_element_type=F32) * g[:, 1:2]
    out = jnp.dot(ha.astype(BF16), w2a[...], preferred_element_type=F32)
    out += jnp.dot(hb.astype(BF16), w2b[...], preferred_element_type=F32)
    o_ref[...] = out


def _expert_blocks(xs, gates, block_ea, block_eb, w1, w3, w2):
    rows = xs.shape[0]
    n_blocks = rows // MOE_ROWS
    up = (None, D_MODEL, D_EXPERT)
    down = (None, D_EXPERT, D_MODEL)
    grid_spec = pltpu.PrefetchScalarGridSpec(
        num_scalar_prefetch=2,
        grid=(n_blocks,),
        in_specs=[pl.BlockSpec((MOE_ROWS, D_MODEL), lambda i, ea, eb: (i, 0)),
                  pl.BlockSpec((MOE_ROWS, 2), lambda i, ea, eb: (i, 0)),
                  pl.BlockSpec(up, lambda i, ea, eb: (ea[i], 0, 0)),
                  pl.BlockSpec(up, lambda i, ea, eb: (ea[i], 0, 0)),
                  pl.BlockSpec(down, lambda i, ea, eb: (ea[i], 0, 0)),
                  pl.BlockSpec(up, lambda i, ea, eb: (eb[i], 0, 0)),
                  pl.BlockSpec(up, lambda i, ea, eb: (eb[i], 0, 0)),
                  pl.BlockSpec(down, lambda i, ea, eb: (eb[i], 0, 0))],
        out_specs=pl.BlockSpec((MOE_ROWS, D_MODEL), lambda i, ea, eb: (i, 0)),
    )
    return pl.pallas_call(
        _moe_kernel,
        grid_spec=grid_spec,
        out_shape=jax.ShapeDtypeStruct((rows, D_MODEL), F32),
        compiler_params=_params(("arbitrary",)),
        name="expert_pair_blocks",
    )(block_ea, block_eb, xs, gates, w1, w3, w2, w1, w3, w2)


def _route(logits, bg, be):
    t = logits.shape[0]
    lg = logits[:, :N_EXPERT_GROUPS] + bg.astype(F32)
    pg = jax.nn.softmax(lg, axis=-1)
    g_star = jnp.argmax(pg, axis=-1).astype(jnp.int32)
    gate1 = jnp.take_along_axis(pg, g_star[:, None], axis=1)[:, 0]
    le = logits[:, N_EXPERT_GROUPS:N_EXPERT_GROUPS + N_EXPERTS].reshape(t, N_EXPERT_GROUPS, EXPERTS_PER_GROUP)
    le = le + be.astype(F32)
    le = jnp.take_along_axis(le, g_star[:, None, None], axis=1)[:, 0]
    top_vals, top_idx = lax.top_k(le, 2)
    gates = gate1[:, None] * jax.nn.softmax(top_vals, axis=-1)
    first_low = top_idx[:, 0] < top_idx[:, 1]
    lo = jnp.minimum(top_idx[:, 0], top_idx[:, 1]).astype(jnp.int32)
    hi = jnp.maximum(top_idx[:, 0], top_idx[:, 1]).astype(jnp.int32)
    g_lo = jnp.where(first_low, gates[:, 0], gates[:, 1])
    g_hi = jnp.where(first_low, gates[:, 1], gates[:, 0])
    pair = lo * (2 * EXPERTS_PER_GROUP - 1 - lo) // 2 + (hi - lo - 1)
    cls = g_star * len(PAIRS) + pair
    return cls, jnp.stack([g_lo, g_hi], axis=1)


def _hierarchical_moe(h, gain, wg, bg, we, be, w1, w3, w2):
    t = h.shape[0]
    wcat = jnp.concatenate(
        [wg, we.transpose(1, 0, 2).reshape(D_MODEL, N_EXPERTS),
         jnp.zeros((D_MODEL, LANES - N_EXPERT_GROUPS - N_EXPERTS), F32)], axis=1)
    y, logits = _router_logits(h, gain, wcat)
    cls, gates = _route(logits, bg, be)

    onehot = (cls[:, None] == jnp.arange(N_CLASSES, dtype=jnp.int32)[None, :]).astype(jnp.int32)
    rank = jnp.take_along_axis(jnp.cumsum(onehot, axis=0), cls[:, None], axis=1)[:, 0] - 1
    counts = jnp.sum(onehot, axis=0)
    padded = ((counts + MOE_ROWS - 1) // MOE_ROWS) * MOE_ROWS
    pends = jnp.cumsum(padded)
    dest = (pends - padded)[cls] + rank
    n_blocks = -(-(t + N_CLASSES * (MOE_ROWS - 1)) // MOE_ROWS)
    rows = n_blocks * MOE_ROWS
    row_tok = jnp.zeros((rows,), jnp.int32).at[dest].set(jnp.arange(t, dtype=jnp.int32))
    block_cls = jnp.minimum(
        jnp.searchsorted(pends, jnp.arange(n_blocks, dtype=jnp.int32) * MOE_ROWS, side="right"),
        N_CLASSES - 1).astype(jnp.int32)
    pair_lo = jnp.asarray([p[0] for p in PAIRS], jnp.int32)
    pair_hi = jnp.asarray([p[1] for p in PAIRS], jnp.int32)
    grp = block_cls // len(PAIRS)
    block_ea = grp * EXPERTS_PER_GROUP + pair_lo[block_cls % len(PAIRS)]
    block_eb = grp * EXPERTS_PER_GROUP + pair_hi[block_cls % len(PAIRS)]

    xs = jnp.take(y, row_tok, axis=0)
    row_gates = jnp.take(gates, row_tok, axis=0)
    ys = _expert_blocks(xs, row_gates, block_ea, block_eb, w1, w3, w2)
    return h + jnp.take(ys, dest, axis=0)


def _norm_proj_kernel(x_ref, g_ref, w_ref, o_ref, y_sc):
    @pl.when(pl.program_id(1) == 0)
    def _():
        y_sc[...] = _rmsnorm_rows(x_ref[...], g_ref[...]).astype(BF16)

    o_ref[...] = jnp.dot(y_sc[...], w_ref[...], preferred_element_type=F32)


def _norm_proj(x, gain, w):
    t, n = x.shape[0], w.shape[1]
    return pl.pallas_call(
        _norm_proj_kernel,
        grid=(t // ROW_TILE, n // COL_TILE),
        in_specs=[pl.BlockSpec((ROW_TILE, D_MODEL), lambda i, j: (i, 0)),
                  pl.BlockSpec((1, D_MODEL), lambda i, j: (0, 0)),
                  pl.BlockSpec((D_MODEL, COL_TILE), lambda i, j: (0, j))],
        out_specs=pl.BlockSpec((ROW_TILE, COL_TILE), lambda i, j: (i, j)),
        out_shape=jax.ShapeDtypeStruct((t, n), F32),
        scratch_shapes=[pltpu.VMEM((ROW_TILE, D_MODEL), BF16)],
        compiler_params=_params(("parallel", "arbitrary")),
        name="norm_proj",
    )(x, gain.reshape(1, D_MODEL), w)


def _proj_residual_kernel(z_ref, w_ref, h_ref, o_ref):
    o_ref[...] = h_ref[...] + jnp.dot(z_ref[...], w_ref[...], preferred_element_type=F32)


def _proj_residual(z, w, h):
    t, n = h.shape
    k = z.shape[1]
    return pl.pallas_call(
        _proj_residual_kernel,
        grid=(t // ROW_TILE, n // COL_TILE),
        in_specs=[pl.BlockSpec((ROW_TILE, k), lambda i, j: (i, 0)),
                  pl.BlockSpec((k, COL_TILE), lambda i, j: (0, j)),
                  pl.BlockSpec((ROW_TILE, COL_TILE), lambda i, j: (i, j))],
        out_specs=pl.BlockSpec((ROW_TILE, COL_TILE), lambda i, j: (i, j)),
        out_shape=jax.ShapeDtypeStruct((t, n), F32),
        compiler_params=_params(("parallel", "arbitrary")),
        name="proj_residual",
    )(z, w, h)


def _pool_kernel(u_ref, wg_ref, sc_ref, z_ref):
    seq = u_ref.shape[0]
    row = lax.broadcasted_iota(jnp.int32, (seq, POOL_GROUP_DIM), 0)
    for g, window in enumerate(POOL_WINDOWS):
        cols = slice(g * POOL_GROUP_DIM, (g + 1) * POOL_GROUP_DIM)
        u = u_ref[:, cols]
        s = u
        step = 1
        while step < window:
            s = s + jnp.where(row >= step, pltpu.roll(s, step, 0), 0.0)
            step *= 2
        count = jnp.minimum(row + 1, window).astype(F32)
        pooled = (s / count - u).astype(BF16)
        z = jnp.dot(pooled, wg_ref[g], preferred_element_type=F32) * sc_ref[:, cols]
        z_ref[:, cols] = z.astype(BF16)


def _pool_mixer_inner(u, w_group, scale, batch, seq):
    blk = pl.BlockSpec((None, seq, D_MODEL), lambda b: (b, 0, 0))
    z = pl.pallas_call(
        _pool_kernel,
        grid=(batch,),
        in_specs=[blk,
                  pl.BlockSpec((N_POOL_GROUPS, POOL_GROUP_DIM, POOL_GROUP_DIM), lambda b: (0, 0, 0)),
                  pl.BlockSpec((1, D_MODEL), lambda b: (0, 0))],
        out_specs=blk,
        out_shape=jax.ShapeDtypeStruct((batch, seq, D_MODEL), BF16),
        compiler_params=_params(("parallel",)),
        name="causal_pool",
    )(u.reshape(batch, seq, D_MODEL), w_group, scale.reshape(1, D_MODEL))
    return z.reshape(batch * seq, D_MODEL)


def _final_norm_kernel(x_ref, g_ref, o_ref):
    o_ref[...] = _rmsnorm_rows(x_ref[...], g_ref[...])


def _final_norm(h, gain):
    t = h.shape[0]
    spec = pl.BlockSpec((ROW_TILE, D_MODEL), lambda i: (i, 0))
    return pl.pallas_call(
        _final_norm_kernel,
        grid=(t // ROW_TILE,),
        in_specs=[spec, pl.BlockSpec((1, D_MODEL), lambda i: (0, 0))],
        out_specs=spec,
        out_shape=jax.ShapeDtypeStruct((t, D_MODEL), F32),
        compiler_params=_params(("parallel",)),
        name="final_norm",
    )(h, gain.reshape(1, D_MODEL))


def kernel(x, positions, norm_mix, norm_ffn, norm_final, attn_w_in, attn_w_out, pool_w_in, pool_w_group, pool_scale, pool_w_out, router_group_w, router_group_b, router_expert_w, router_expert_b, expert_w1, expert_w3, expert_w2):
    batch, seq, _ = x.shape
    t = batch * seq
    h = x.reshape(t, D_MODEL)
    w1 = expert_w1.astype(BF16)
    w3 = expert_w3.astype(BF16)
    w2 = expert_w2.astype(BF16)

    c, s1, s2 = _rope_tables(positions)
    qkv = _qkv_proj(h, norm_mix[0], attn_w_in[0].astype(BF16), c, s1, s2)
    outs, lses = [], []
    for g in range(N_ATTN_GROUPS):
        o, lse = _dilated_attention(qkv, g, batch, seq)
        outs.append(o)
        lses.append(lse)
    h = _merge_out_proj(outs, lses, attn_w_out[0].astype(BF16), h)
    h = _hierarchical_moe(h, norm_ffn[0], router_group_w[0], router_group_b[0],
                          router_expert_w[0], router_expert_b[0], w1[0], w3[0], w2[0])

    u = _norm_proj(h, norm_mix[1], pool_w_in[0].astype(BF16))
    z = _pool_mixer_inner(u, pool_w_group[0].astype(BF16), pool_scale[0], batch, seq)
    h = _proj_residual(z, pool_w_out[0].astype(BF16), h)
    h = _hierarchical_moe(h, norm_ffn[1], router_group_w[1], router_group_b[1],
                          router_expert_w[1], router_expert_b[1], w1[1], w3[1], w2[1])

    return _final_norm(h, norm_final).reshape(batch, seq, D_MODEL)
```

```python
import functools
import math

import jax
import jax.numpy as jnp
from jax import lax
from jax.experimental import pallas as pl
from jax.experimental.pallas import tpu as pltpu

F32 = jnp.float32
BF16 = jnp.bfloat16

D_MODEL = 1024
ATTN_PATTERNS = ((128, 1), (512, 4), (2048, 16))
N_ATTN_GROUPS = len(ATTN_PATTERNS)
HEADS = 8
HEAD_DIM = 64
ATTN_WIDTH = HEADS * HEAD_DIM
QKV_WIDTH = N_ATTN_GROUPS * 3 * ATTN_WIDTH
ROT_DIM = HEAD_DIM // 4
ROPE_THETA = 500000.0
ATTN_BLOCK = 128
POOL_WINDOWS = (2, 4, 8, 16)
N_POOL_GROUPS = len(POOL_WINDOWS)
POOL_GROUP_DIM = D_MODEL // N_POOL_GROUPS
N_EXPERT_GROUPS = 4
EXPERTS_PER_GROUP = 4
N_EXPERTS = N_EXPERT_GROUPS * EXPERTS_PER_GROUP
D_EXPERT = D_MODEL // 2
RMS_EPS = 1e-6

PAIRS = ((0, 1), (0, 2), (0, 3), (1, 2), (1, 3), (2, 3))
N_CLASSES = N_EXPERT_GROUPS * len(PAIRS)

ROW_TILE = 512
COL_TILE = 512
MOE_ROWS = 256
LANES = 128
VMEM_LIMIT = 48 * 1024 * 1024
MASKED = -1e30


def _params(sem):
    return pltpu.CompilerParams(dimension_semantics=sem, vmem_limit_bytes=VMEM_LIMIT)


def _rmsnorm_rows(x, g):
    ms = jnp.mean(x * x, axis=-1, keepdims=True)
    return (x * lax.rsqrt(ms + RMS_EPS)) * g


def _rope_kernel(pos_ref, freq_ref, c_ref, s1_ref, s2_ref):
    ang = pos_ref[...].astype(F32) * freq_ref[...]
    lane = lax.broadcasted_iota(jnp.int32, ang.shape, 1) % HEAD_DIM
    sin = jnp.sin(ang)
    c_ref[...] = jnp.cos(ang)
    s1_ref[...] = jnp.where(lane < ROT_DIM // 2, -sin, 0.0)
    s2_ref[...] = jnp.where(lane >= ROT_DIM // 2, sin, 0.0)


def _rope_tables(positions):
    t = positions.size
    half = ROT_DIM // 2
    inv = [ROPE_THETA ** (-(i * 2.0 / ROT_DIM)) for i in range(half)]
    per_head = inv + inv + [0.0] * (HEAD_DIM - ROT_DIM)
    freq = jnp.asarray([per_head * (LANES // HEAD_DIM)], F32)
    tm = 1024
    out = jax.ShapeDtypeStruct((t, LANES), F32)
    spec = pl.BlockSpec((tm, LANES), lambda i: (i, 0))
    return pl.pallas_call(
        _rope_kernel,
        grid=(t // tm,),
        in_specs=[pl.BlockSpec((tm, 1), lambda i: (i, 0)),
                  pl.BlockSpec((1, LANES), lambda i: (0, 0))],
        out_specs=[spec, spec, spec],
        out_shape=[out, out, out],
        compiler_params=_params(("parallel",)),
        name="rope_tables",
    )(positions.reshape(t, 1), freq)


def _store_by_phase(res, o_ref, stage, dil):
    if dil == 1:
        o_ref[...] = res.astype(BF16)
        return
    chunks = COL_TILE // LANES
    for c in range(chunks):
        stage[c] = res[:, c * LANES:(c + 1) * LANES]
    n = ROW_TILE // dil
    for p in range(dil):
        rows = [stage[c, pl.ds(p, n, stride=dil), :] for c in range(chunks)]
        o_ref[:, p * COL_TILE:(p + 1) * COL_TILE] = jnp.concatenate(rows, axis=1).astype(BF16)


def _qkv_kernel(x_ref, g_ref, w_ref, c_ref, s1_ref, s2_ref, *rest):
    o_refs, (y_sc, stage) = rest[:-2], rest[-2:]
    j = pl.program_id(1)

    @pl.when(j == 0)
    def _():
        y_sc[...] = _rmsnorm_rows(x_ref[...], g_ref[...]).astype(BF16)

    acc = jnp.dot(y_sc[...], w_ref[...], preferred_element_type=F32)
    for col in range(QKV_WIDTH // COL_TILE):
        dil = ATTN_PATTERNS[col // 3][1]

        @pl.when(j == col)
        def _(col=col, dil=dil):
            if col % 3 == 2:
                _store_by_phase(acc, o_refs[col], stage, dil)
                return
            reps = COL_TILE // LANES
            c = jnp.concatenate([c_ref[...]] * reps, axis=1)
            s1 = jnp.concatenate([s1_ref[...]] * reps, axis=1)
            s2 = jnp.concatenate([s2_ref[...]] * reps, axis=1)
            up = pltpu.roll(acc, COL_TILE - ROT_DIM // 2, 1)
            dn = pltpu.roll(acc, ROT_DIM // 2, 1)
            _store_by_phase(acc * c + up * s1 + dn * s2, o_refs[col], stage, dil)


def _qkv_proj(x, gain, w, c, s1, s2, batch, seq):
    t = x.shape[0]
    tiles = seq // ROW_TILE
    tab = pl.BlockSpec((ROW_TILE, LANES), lambda i, j: (i, 0))
    out_specs, out_shapes = [], []
    for _, dil in ATTN_PATTERNS:
        for _ in range(3):
            out_specs.append(pl.BlockSpec((None, ROW_TILE // dil, dil * ATTN_WIDTH),
                                          lambda i, j: (i // tiles, i % tiles, 0)))
            out_shapes.append(jax.ShapeDtypeStruct((batch, seq // dil, dil * ATTN_WIDTH), BF16))
    return pl.pallas_call(
        _qkv_kernel,
        grid=(t // ROW_TILE, QKV_WIDTH // COL_TILE),
        in_specs=[pl.BlockSpec((ROW_TILE, D_MODEL), lambda i, j: (i, 0)),
                  pl.BlockSpec((1, D_MODEL), lambda i, j: (0, 0)),
                  pl.BlockSpec((D_MODEL, COL_TILE), lambda i, j: (0, j)),
                  tab, tab, tab],
        out_specs=out_specs,
        out_shape=out_shapes,
        scratch_shapes=[pltpu.VMEM((ROW_TILE, D_MODEL), BF16),
                        pltpu.VMEM((COL_TILE // LANES, ROW_TILE, LANES), F32)],
        compiler_params=_params(("parallel", "arbitrary")),
        name="qkv_proj",
    )(x, gain.reshape(1, D_MODEL), w, c, s1, s2)


def _attn_kernel(q_ref, kp_ref, kc_ref, vp_ref, vc_ref, o_ref, l_ref):
    j = pl.program_id(2)
    blk = ATTN_BLOCK
    row = lax.broadcasted_iota(jnp.int32, (blk, 2 * blk), 0)
    col = lax.broadcasted_iota(jnp.int32, (blk, 2 * blk), 1)
    lo = jnp.maximum(row, jnp.where(j == 0, blk, 0))
    bias = jnp.where(col >= lo, jnp.where(col <= row + blk, 0.0, MASKED), MASKED)
    scale = 1.0 / math.sqrt(HEAD_DIM)
    outs, lses = [], []
    for h in range(HEADS):
        sl = slice(h * HEAD_DIM, (h + 1) * HEAD_DIM)
        q = q_ref[:, sl]
        k = jnp.concatenate([kp_ref[:, sl], kc_ref[:, sl]], axis=0)
        v = jnp.concatenate([vp_ref[:, sl], vc_ref[:, sl]], axis=0)
        s = lax.dot_general(q, k, (((1,), (1,)), ((), ())), preferred_element_type=F32)
        s = s * scale + bias
        m = jnp.max(s, axis=-1, keepdims=True)
        p = jnp.exp(s - m)
        den = jnp.sum(p, axis=-1, keepdims=True)
        o = jnp.dot(p.astype(BF16), v, preferred_element_type=F32) / den
        outs.append(o)
        lses.append(jnp.broadcast_to(m + jnp.log(den), (blk, HEAD_DIM)))
    o_ref[...] = jnp.concatenate(outs, axis=1).astype(BF16)
    l_ref[...] = jnp.concatenate(lses, axis=1)


def _dilated_attention(q, k, v, group, batch, seq):
    _, dil = ATTN_PATTERNS[group]
    length = seq // dil
    nb = length // ATTN_BLOCK
    cur = pl.BlockSpec((None, ATTN_BLOCK, ATTN_WIDTH), lambda b, p, j: (b, j, p))
    prev = pl.BlockSpec((None, ATTN_BLOCK, ATTN_WIDTH), lambda b, p, j: (b, jnp.maximum(j - 1, 0), p))
    return pl.pallas_call(
        _attn_kernel,
        grid=(batch, dil, nb),
        in_specs=[cur, prev, cur, prev, cur],
        out_specs=[cur, cur],
        out_shape=[jax.ShapeDtypeStruct((batch, length, dil * ATTN_WIDTH), BF16),
                   jax.ShapeDtypeStruct((batch, length, dil * ATTN_WIDTH), F32)],
        compiler_params=_params(("parallel", "parallel", "arbitrary")),
        name=f"dilated_attn_g{group}",
    )(q, k, k, v, v)


def _natural_rows(ref, stage, dil):
    if dil == 1:
        return ref[...].astype(F32)
    n = ROW_TILE // dil
    chunks = ATTN_WIDTH // LANES
    for p in range(dil):
        for c in range(chunks):
            lanes = slice(p * ATTN_WIDTH + c * LANES, p * ATTN_WIDTH + (c + 1) * LANES)
            stage[c, pl.ds(p, n, stride=dil), :] = ref[:, lanes].astype(F32)
    return jnp.concatenate([stage[c] for c in range(chunks)], axis=1)


def _merge_kernel(o0, o1, o2, l0, l1, l2, w_ref, h_ref, out_ref, so1, so2, sl1, sl2):
    dils = [d for _, d in ATTN_PATTERNS]
    a = _natural_rows(l0, None, dils[0])
    b = _natural_rows(l1, sl1, dils[1])
    c = _natural_rows(l2, sl2, dils[2])
    m = jnp.maximum(jnp.maximum(a, b), c)
    ea, eb, ec = jnp.exp(a - m), jnp.exp(b - m), jnp.exp(c - m)
    mix = ea * _natural_rows(o0, None, dils[0])
    mix += eb * _natural_rows(o1, so1, dils[1])
    mix += ec * _natural_rows(o2, so2, dils[2])
    mix = mix / (ea + eb + ec)
    out_ref[...] = h_ref[...] + jnp.dot(mix.astype(BF16), w_ref[...], preferred_element_type=F32)


def _merge_out_proj(outs, lses, w, h, seq):
    t = h.shape[0]
    tiles = seq // ROW_TILE
    slabs = [pl.BlockSpec((None, ROW_TILE // d, d * ATTN_WIDTH), lambda i: (i // tiles, i % tiles, 0))
             for _, d in ATTN_PATTERNS]
    full = pl.BlockSpec((ROW_TILE, D_MODEL), lambda i: (i, 0))
    stage = pltpu.VMEM((ATTN_WIDTH // LANES, ROW_TILE, LANES), F32)
    return pl.pallas_call(
        _merge_kernel,
        grid=(t // ROW_TILE,),
        in_specs=slabs + slabs + [pl.BlockSpec((ATTN_WIDTH, D_MODEL), lambda i: (0, 0)), full],
        out_specs=full,
        out_shape=jax.ShapeDtypeStruct((t, D_MODEL), F32),
        scratch_shapes=[stage, stage, stage, stage],
        compiler_params=_params(("parallel",)),
        name="attn_merge_out",
    )(*outs, *lses, w, h)


def _router_kernel(x_ref, g_ref, whi_ref, wlo_ref, y_ref, lg_ref):
    y = _rmsnorm_rows(x_ref[...], g_ref[...])
    y_hi = y.astype(BF16)
    y_lo = (y - y_hi.astype(F32)).astype(BF16)
    whi = whi_ref[...]
    lg = jnp.dot(y_hi, whi, preferred_element_type=F32)
    lg += jnp.dot(y_lo, whi, preferred_element_type=F32)
    lg += jnp.dot(y_hi, wlo_ref[...], preferred_element_type=F32)
    y_ref[...] = y_hi
    lg_ref[...] = lg


def _router_logits(h, gain, wcat):
    t = h.shape[0]
    whi = wcat.astype(BF16)
    wlo = (wcat - whi.astype(F32)).astype(BF16)
    wspec = pl.BlockSpec((D_MODEL, LANES), lambda i: (0, 0))
    return pl.pallas_call(
        _router_kernel,
        grid=(t // ROW_TILE,),
        in_specs=[pl.BlockSpec((ROW_TILE, D_MODEL), lambda i: (i, 0)),
                  pl.BlockSpec((1, D_MODEL), lambda i: (0, 0)), wspec, wspec],
        out_specs=[pl.BlockSpec((ROW_TILE, D_MODEL), lambda i: (i, 0)),
                   pl.BlockSpec((ROW_TILE, LANES), lambda i: (i, 0))],
        out_shape=[jax.ShapeDtypeStruct((t, D_MODEL), BF16),
                   jax.ShapeDtypeStruct((t, LANES), F32)],
        compiler_params=_params(("parallel",)),
        name="router_logits",
    )(h, gain.reshape(1, D_MODEL), whi, wlo)


def _silu(a):
    return a / (1.0 + jnp.exp(-a))


def _moe_kernel(ea_ref, eb_ref, x_ref, g_ref, w1a, w3a, w2a, w1b, w3b, w2b, o_ref):
    del ea_ref, eb_ref
    x = x_ref[...]
    g = g_ref[...]
    ha = _silu(jnp.dot(x, w1a[...], preferred_element_type=F32))
    ha = ha * jnp.dot(x, w3a[...], preferred_element_type=F32) * g[:, 0:1]
    hb = _silu(jnp.dot(x, w1b[...], preferred_element_type=F32))
    hb = hb * jnp.dot(x, w3b[...], preferred_element_type=F32) * g[:, 1:2]
    out = jnp.dot(ha.astype(BF16), w2a[...], preferred_element_type=F32)
    out += jnp.dot(hb.astype(BF16), w2b[...], preferred_element_type=F32)
    o_ref[...] = out


def _expert_blocks(xs, gates, block_ea, block_eb, w1, w3, w2):
    rows = xs.shape[0]
    n_blocks = rows // MOE_ROWS
    up = (None, D_MODEL, D_EXPERT)
    down = (None, D_EXPERT, D_MODEL)
    grid_spec = pltpu.PrefetchScalarGridSpec(
        num_scalar_prefetch=2,
        grid=(n_blocks,),
        in_specs=[pl.BlockSpec((MOE_ROWS, D_MODEL), lambda i, ea, eb: (i, 0)),
                  pl.BlockSpec((MOE_ROWS, 2), lambda i, ea, eb: (i, 0)),
                  pl.BlockSpec(up, lambda i, ea, eb: (ea[i], 0, 0)),
                  pl.BlockSpec(up, lambda i, ea, eb: (ea[i], 0, 0)),
                  pl.BlockSpec(down, lambda i, ea, eb: (ea[i], 0, 0)),
                  pl.BlockSpec(up, lambda i, ea, eb: (eb[i], 0, 0)),
                  pl.BlockSpec(up, lambda i, ea, eb: (eb[i], 0, 0)),
                  pl.BlockSpec(down, lambda i, ea, eb: (eb[i], 0, 0))],
        out_specs=pl.BlockSpec((MOE_ROWS, D_MODEL), lambda i, ea, eb: (i, 0)),
    )
    return pl.pallas_call(
        _moe_kernel,
        grid_spec=grid_spec,
        out_shape=jax.ShapeDtypeStruct((rows, D_MODEL), F32),
        compiler_params=_params(("arbitrary",)),
        name="expert_pair_blocks",
    )(block_ea, block_eb, xs, gates, w1, w3, w2, w1, w3, w2)


def _route(logits, bg, be):
    t = logits.shape[0]
    lg = logits[:, :N_EXPERT_GROUPS] + bg.astype(F32)
    pg = jax.nn.softmax(lg, axis=-1)
    g_star = jnp.argmax(pg, axis=-1).astype(jnp.int32)
    gate1 = jnp.take_along_axis(pg, g_star[:, None], axis=1)[:, 0]
    le = logits[:, N_EXPERT_GROUPS:N_EXPERT_GROUPS + N_EXPERTS].reshape(t, N_EXPERT_GROUPS, EXPERTS_PER_GROUP)
    le = le + be.astype(F32)
    le = jnp.take_along_axis(le, g_star[:, None, None], axis=1)[:, 0]
    top_vals, top_idx = lax.top_k(le, 2)
    gates = gate1[:, None] * jax.nn.softmax(top_vals, axis=-1)
    first_low = top_idx[:, 0] < top_idx[:, 1]
    lo = jnp.minimum(top_idx[:, 0], top_idx[:, 1]).astype(jnp.int32)
    hi = jnp.maximum(top_idx[:, 0], top_idx[:, 1]).astype(jnp.int32)
    g_lo = jnp.where(first_low, gates[:, 0], gates[:, 1])
    g_hi = jnp.where(first_low, gates[:, 1], gates[:, 0])
    pair = lo * (2 * EXPERTS_PER_GROUP - 1 - lo) // 2 + (hi - lo - 1)
    cls = g_star * len(PAIRS) + pair
    return cls, jnp.stack([g_lo, g_hi], axis=1)


def _hierarchical_moe(h, gain, wg, bg, we, be, w1, w3, w2):
    t = h.shape[0]
    wcat = jnp.concatenate(
        [wg, we.transpose(1, 0, 2).reshape(D_MODEL, N_EXPERTS),
         jnp.zeros((D_MODEL, LANES - N_EXPERT_GROUPS - N_EXPERTS), F32)], axis=1)
    y, logits = _router_logits(h, gain, wcat)
    cls, gates = _route(logits, bg, be)

    onehot = (cls[:, None] == jnp.arange(N_CLASSES, dtype=jnp.int32)[None, :]).astype(jnp.int32)
    rank = jnp.take_along_axis(jnp.cumsum(onehot, axis=0), cls[:, None], axis=1)[:, 0] - 1
    counts = jnp.sum(onehot, axis=0)
    padded = ((counts + MOE_ROWS - 1) // MOE_ROWS) * MOE_ROWS
    pends = jnp.cumsum(padded)
    dest = (pends - padded)[cls] + rank
    n_blocks = -(-(t + N_CLASSES * (MOE_ROWS - 1)) // MOE_ROWS)
    rows = n_blocks * MOE_ROWS
    row_tok = jnp.zeros((rows,), jnp.int32).at[dest].set(jnp.arange(t, dtype=jnp.int32))
    block_cls = jnp.minimum(
        jnp.searchsorted(pends, jnp.arange(n_blocks, dtype=jnp.int32) * MOE_ROWS, side="right"),
        N_CLASSES - 1).astype(jnp.int32)
    pair_lo = jnp.asarray([p[0] for p in PAIRS], jnp.int32)
    pair_hi = jnp.asarray([p[1] for p in PAIRS], jnp.int32)
    grp = block_cls // len(PAIRS)
    block_ea = grp * EXPERTS_PER_GROUP + pair_lo[block_cls % len(PAIRS)]
    block_eb = grp * EXPERTS_PER_GROUP + pair_hi[block_cls % len(PAIRS)]

    xs = jnp.take(y, row_tok, axis=0)
    row_gates = jnp.take(gates, row_tok, axis=0)
    ys = _expert_blocks(xs, row_gates, block_ea, block_eb, w1, w3, w2)
    return h + jnp.take(ys, dest, axis=0)


def _norm_proj_kernel(x_ref, g_ref, w_ref, o_ref, y_sc):
    @pl.when(pl.program_id(1) == 0)
    def _():
        y_sc[...] = _rmsnorm_rows(x_ref[...], g_ref[...]).astype(BF16)

    o_ref[...] = jnp.dot(y_sc[...], w_ref[...], preferred_element_type=F32)


def _norm_proj(x, gain, w):
    t, n = x.shape[0], w.shape[1]
    return pl.pallas_call(
        _norm_proj_kernel,
        grid=(t // ROW_TILE, n // COL_TILE),
        in_specs=[pl.BlockSpec((ROW_TILE, D_MODEL), lambda i, j: (i, 0)),
                  pl.BlockSpec((1, D_MODEL), lambda i, j: (0, 0)),
                  pl.BlockSpec((D_MODEL, COL_TILE), lambda i, j: (0, j))],
        out_specs=pl.BlockSpec((ROW_TILE, COL_TILE), lambda i, j: (i, j)),
        out_shape=jax.ShapeDtypeStruct((t, n), F32),
        scratch_shapes=[pltpu.VMEM((ROW_TILE, D_MODEL), BF16)],
        compiler_params=_params(("parallel", "arbitrary")),
        name="norm_proj",
    )(x, gain.reshape(1, D_MODEL), w)


def _proj_residual_kernel(z_ref, w_ref, h_ref, o_ref):
    o_ref[...] = h_ref[...] + jnp.dot(z_ref[...], w_ref[...], preferred_element_type=F32)


def _proj_residual(z, w, h):
    t, n = h.shape
    k = z.shape[1]
    return pl.pallas_call(
        _proj_residual_kernel,
        grid=(t // ROW_TILE, n // COL_TILE),
        in_specs=[pl.BlockSpec((ROW_TILE, k), lambda i, j: (i, 0)),
                  pl.BlockSpec((k, COL_TILE), lambda i, j: (0, j)),
                  pl.BlockSpec((ROW_TILE, COL_TILE), lambda i, j: (i, j))],
        out_specs=pl.BlockSpec((ROW_TILE, COL_TILE), lambda i, j: (i, j)),
        out_shape=jax.ShapeDtypeStruct((t, n), F32),
        compiler_params=_params(("parallel", "arbitrary")),
        name="proj_residual",
    )(z, w, h)


def _pool_kernel(u_ref, wg_ref, sc_ref, z_ref):
    seq = u_ref.shape[0]
    row = lax.broadcasted_iota(jnp.int32, (seq, POOL_GROUP_DIM), 0)
    for g, window in enumerate(POOL_WINDOWS):
        cols = slice(g * POOL_GROUP_DIM, (g + 1) * POOL_GROUP_DIM)
        u = u_ref[:, cols]
        s = u
        step = 1
        while step < window:
            s = s + jnp.where(row >= step, pltpu.roll(s, step, 0), 0.0)
            step *= 2
        count = jnp.minimum(row + 1, window).astype(F32)
        pooled = (s / count - u).astype(BF16)
        z = jnp.dot(pooled, wg_ref[g], preferred_element_type=F32) * sc_ref[:, cols]
        z_ref[:, cols] = z.astype(BF16)


def _pool_mixer_inner(u, w_group, scale, batch, seq):
    blk = pl.BlockSpec((None, seq, D_MODEL), lambda b: (b, 0, 0))
    z = pl.pallas_call(
        _pool_kernel,
        grid=(batch,),
        in_specs=[blk,
                  pl.BlockSpec((N_POOL_GROUPS, POOL_GROUP_DIM, POOL_GROUP_DIM), lambda b: (0, 0, 0)),
                  pl.BlockSpec((1, D_MODEL), lambda b: (0, 0))],
        out_specs=blk,
        out_shape=jax.ShapeDtypeStruct((batch, seq, D_MODEL), BF16),
        compiler_params=_params(("parallel",)),
        name="causal_pool",
    )(u.reshape(batch, seq, D_MODEL), w_group, scale.reshape(1, D_MODEL))
    return z.reshape(batch * seq, D_MODEL)


def _final_norm_kernel(x_ref, g_ref, o_ref):
    o_ref[...] = _rmsnorm_rows(x_ref[...], g_ref[...])


def _final_norm(h, gain):
    t = h.shape[0]
    spec = pl.BlockSpec((ROW_TILE, D_MODEL), lambda i: (i, 0))
    return pl.pallas_call(
        _final_norm_kernel,
        grid=(t // ROW_TILE,),
        in_specs=[spec, pl.BlockSpec((1, D_MODEL), lambda i: (0, 0))],
        out_specs=spec,
        out_shape=jax.ShapeDtypeStruct((t, D_MODEL), F32),
        compiler_params=_params(("parallel",)),
        name="final_norm",
    )(h, gain.reshape(1, D_MODEL))


def kernel(x, positions, norm_mix, norm_ffn, norm_final, attn_w_in, attn_w_out, pool_w_in, pool_w_group, pool_scale, pool_w_out, router_group_w, router_group_b, router_expert_w, router_expert_b, expert_w1, expert_w3, expert_w2):
    batch, seq, _ = x.shape
    t = batch * seq
    h = x.reshape(t, D_MODEL)
    w1 = expert_w1.astype(BF16)
    w3 = expert_w3.astype(BF16)
    w2 = expert_w2.astype(BF16)

    c, s1, s2 = _rope_tables(positions)
    qkv = _qkv_proj(h, norm_mix[0], attn_w_in[0].astype(BF16), c, s1, s2, batch, seq)
    outs, lses = [], []
    for g in range(N_ATTN_GROUPS):
        o, lse = _dilated_attention(qkv[3 * g], qkv[3 * g + 1], qkv[3 * g + 2], g, batch, seq)
        outs.append(o)
        lses.append(lse)
    h = _merge_out_proj(outs, lses, attn_w_out[0].astype(BF16), h, seq)
    h = _hierarchical_moe(h, norm_ffn[0], router_group_w[0], router_group_b[0],
                          router_expert_w[0], router_expert_b[0], w1[0], w3[0], w2[0])

    u = _norm_proj(h, norm_mix[1], pool_w_in[0].astype(BF16))
    z = _pool_mixer_inner(u, pool_w_group[0].astype(BF16), pool_scale[0], batch, seq)
    h = _proj_residual(z, pool_w_out[0].astype(BF16), h)
    h = _hierarchical_moe(h, norm_ffn[1], router_group_w[1], router_group_b[1],
                          router_expert_w[1], router_expert_b[1], w1[1], w3[1], w2[1])

    return _final_norm(h, norm_final).reshape(batch, seq, D_MODEL)
```

```python
import functools
import math

import jax
import jax.numpy as jnp
from jax import lax
from jax.experimental import pallas as pl
from jax.experimental.pallas import tpu as pltpu

F32 = jnp.float32
BF16 = jnp.bfloat16

D_MODEL = 1024
ATTN_PATTERNS = ((128, 1), (512, 4), (2048, 16))
N_ATTN_GROUPS = len(ATTN_PATTERNS)
HEADS = 8
HEAD_DIM = 64
ATTN_WIDTH = HEADS * HEAD_DIM
QKV_WIDTH = N_ATTN_GROUPS * 3 * ATTN_WIDTH
ROT_DIM = HEAD_DIM // 4
ROPE_THETA = 500000.0
ATTN_BLOCK = 128
POOL_WINDOWS = (2, 4, 8, 16)
N_POOL_GROUPS = len(POOL_WINDOWS)
POOL_GROUP_DIM = D_MODEL // N_POOL_GROUPS
N_EXPERT_GROUPS = 4
EXPERTS_PER_GROUP = 4
N_EXPERTS = N_EXPERT_GROUPS * EXPERTS_PER_GROUP
D_EXPERT = D_MODEL // 2
RMS_EPS = 1e-6

PAIRS = ((0, 1), (0, 2), (0, 3), (1, 2), (1, 3), (2, 3))
N_CLASSES = N_EXPERT_GROUPS * len(PAIRS)

ROW_TILE = 512
COL_TILE = 512
MOE_ROWS = 256
LANES = 128
VMEM_LIMIT = 48 * 1024 * 1024
MASKED = -1e30


def _params(sem):
    return pltpu.CompilerParams(dimension_semantics=sem, vmem_limit_bytes=VMEM_LIMIT)


def _rmsnorm_rows(x, g):
    ms = jnp.mean(x * x, axis=-1, keepdims=True)
    return (x * lax.rsqrt(ms + RMS_EPS)) * g


def _rope_kernel(pos_ref, freq_ref, c_ref, s1_ref, s2_ref):
    ang = pos_ref[...].astype(F32) * freq_ref[...]
    lane = lax.broadcasted_iota(jnp.int32, ang.shape, 1) % HEAD_DIM
    sin = jnp.sin(ang)
    c_ref[...] = jnp.cos(ang)
    s1_ref[...] = jnp.where(lane < ROT_DIM // 2, -sin, 0.0)
    s2_ref[...] = jnp.where(lane >= ROT_DIM // 2, sin, 0.0)


def _rope_tables(positions):
    t = positions.size
    half = ROT_DIM // 2
    inv = [ROPE_THETA ** (-(i * 2.0 / ROT_DIM)) for i in range(half)]
    per_head = inv + inv + [0.0] * (HEAD_DIM - ROT_DIM)
    freq = jnp.asarray([per_head * (LANES // HEAD_DIM)], F32)
    tm = 1024
    out = jax.ShapeDtypeStruct((t, LANES), F32)
    spec = pl.BlockSpec((tm, LANES), lambda i: (i, 0))
    return pl.pallas_call(
        _rope_kernel,
        grid=(t // tm,),
        in_specs=[pl.BlockSpec((tm, 1), lambda i: (i, 0)),
                  pl.BlockSpec((1, LANES), lambda i: (0, 0))],
        out_specs=[spec, spec, spec],
        out_shape=[out, out, out],
        compiler_params=_params(("parallel",)),
        name="rope_tables",
    )(positions.reshape(t, 1), freq)


def _qkv_kernel(x_ref, g_ref, w_ref, c_ref, s1_ref, s2_ref, *rest):
    o_refs, (y_sc, acc_sc) = rest[:-2], rest[-2:]
    y_sc[...] = _rmsnorm_rows(x_ref[...], g_ref[...]).astype(BF16)
    chunks = COL_TILE // LANES
    for col in range(QKV_WIDTH // COL_TILE):
        dil = ATTN_PATTERNS[col // 3][1]
        n = ROW_TILE // dil
        buf = col % 2
        acc = jnp.dot(y_sc[...], w_ref[:, col * COL_TILE:(col + 1) * COL_TILE],
                      preferred_element_type=F32)
        for c in range(chunks):
            acc_sc[buf, c] = acc[:, c * LANES:(c + 1) * LANES]
        rotary = col % 3 != 2
        for p in range(dil):
            rows = pl.ds(p, n, stride=dil) if dil > 1 else slice(None)
            if rotary:
                cos, sin_up, sin_dn = c_ref[rows, :], s1_ref[rows, :], s2_ref[rows, :]
            for c in range(chunks):
                a = acc_sc[buf, c, rows, :]
                if rotary:
                    a = (a * cos + pltpu.roll(a, LANES - ROT_DIM // 2, 1) * sin_up
                         + pltpu.roll(a, ROT_DIM // 2, 1) * sin_dn)
                lanes = slice(p * COL_TILE + c * LANES, p * COL_TILE + (c + 1) * LANES)
                o_refs[col][:, lanes] = a.astype(BF16)


def _qkv_proj(x, gain, w, c, s1, s2, batch, seq):
    t = x.shape[0]
    tiles = seq // ROW_TILE
    tab = pl.BlockSpec((ROW_TILE, LANES), lambda i: (i, 0))
    out_specs, out_shapes = [], []
    for _, dil in ATTN_PATTERNS:
        for _ in range(3):
            out_specs.append(pl.BlockSpec((None, ROW_TILE // dil, dil * ATTN_WIDTH),
                                          lambda i: (i // tiles, i % tiles, 0)))
            out_shapes.append(jax.ShapeDtypeStruct((batch, seq // dil, dil * ATTN_WIDTH), BF16))
    return pl.pallas_call(
        _qkv_kernel,
        grid=(t // ROW_TILE,),
        in_specs=[pl.BlockSpec((ROW_TILE, D_MODEL), lambda i: (i, 0)),
                  pl.BlockSpec((1, D_MODEL), lambda i: (0, 0)),
                  pl.BlockSpec((D_MODEL, QKV_WIDTH), lambda i: (0, 0)),
                  tab, tab, tab],
        out_specs=out_specs,
        out_shape=out_shapes,
        scratch_shapes=[pltpu.VMEM((ROW_TILE, D_MODEL), BF16),
                        pltpu.VMEM((2, COL_TILE // LANES, ROW_TILE, LANES), F32)],
        compiler_params=_params(("parallel",)),
        name="qkv_proj",
    )(x, gain.reshape(1, D_MODEL), w, c, s1, s2)


def _attn_kernel(q_ref, kp_ref, kc_ref, vp_ref, vc_ref, o_ref, l_ref):
    j = pl.program_id(2)
    blk = ATTN_BLOCK
    row = lax.broadcasted_iota(jnp.int32, (blk, 2 * blk), 0)
    col = lax.broadcasted_iota(jnp.int32, (blk, 2 * blk), 1)
    lo = jnp.maximum(row, jnp.where(j == 0, blk, 0))
    bias = jnp.where(col >= lo, jnp.where(col <= row + blk, 0.0, MASKED), MASKED)
    scale = 1.0 / math.sqrt(HEAD_DIM)
    outs, lses = [], []
    for h in range(HEADS):
        sl = slice(h * HEAD_DIM, (h + 1) * HEAD_DIM)
        q = q_ref[:, sl]
        k = jnp.concatenate([kp_ref[:, sl], kc_ref[:, sl]], axis=0)
        v = jnp.concatenate([vp_ref[:, sl], vc_ref[:, sl]], axis=0)
        s = lax.dot_general(q, k, (((1,), (1,)), ((), ())), preferred_element_type=F32)
        s = s * scale + bias
        m = jnp.max(s, axis=-1, keepdims=True)
        p = jnp.exp(s - m)
        den = jnp.sum(p, axis=-1, keepdims=True)
        o = jnp.dot(p.astype(BF16), v, preferred_element_type=F32) / den
        outs.append(o)
        lses.append(jnp.broadcast_to(m + jnp.log(den), (blk, HEAD_DIM)))
    o_ref[...] = jnp.concatenate(outs, axis=1).astype(BF16)
    l_ref[...] = jnp.concatenate(lses, axis=1)


def _dilated_attention(q, k, v, group, batch, seq):
    _, dil = ATTN_PATTERNS[group]
    length = seq // dil
    nb = length // ATTN_BLOCK
    cur = pl.BlockSpec((None, ATTN_BLOCK, ATTN_WIDTH), lambda b, p, j: (b, j, p))
    prev = pl.BlockSpec((None, ATTN_BLOCK, ATTN_WIDTH), lambda b, p, j: (b, jnp.maximum(j - 1, 0), p))
    return pl.pallas_call(
        _attn_kernel,
        grid=(batch, dil, nb),
        in_specs=[cur, prev, cur, prev, cur],
        out_specs=[cur, cur],
        out_shape=[jax.ShapeDtypeStruct((batch, length, dil * ATTN_WIDTH), BF16),
                   jax.ShapeDtypeStruct((batch, length, dil * ATTN_WIDTH), F32)],
        compiler_params=_params(("parallel", "parallel", "arbitrary")),
        name=f"dilated_attn_g{group}",
    )(q, k, k, v, v)


def _natural_rows(ref, stage, dil):
    if dil == 1:
        return ref[...].astype(F32)
    n = ROW_TILE // dil
    chunks = ATTN_WIDTH // LANES
    for p in range(dil):
        for c in range(chunks):
            lanes = slice(p * ATTN_WIDTH + c * LANES, p * ATTN_WIDTH + (c + 1) * LANES)
            stage[c, pl.ds(p, n, stride=dil), :] = ref[:, lanes].astype(F32)
    return jnp.concatenate([stage[c] for c in range(chunks)], axis=1)


def _merge_kernel(o0, o1, o2, l0, l1, l2, w_ref, h_ref, out_ref, so1, so2, sl1, sl2):
    dils = [d for _, d in ATTN_PATTERNS]
    a = _natural_rows(l0, None, dils[0])
    b = _natural_rows(l1, sl1, dils[1])
    c = _natural_rows(l2, sl2, dils[2])
    m = jnp.maximum(jnp.maximum(a, b), c)
    ea, eb, ec = jnp.exp(a - m), jnp.exp(b - m), jnp.exp(c - m)
    mix = ea * _natural_rows(o0, None, dils[0])
    mix += eb * _natural_rows(o1, so1, dils[1])
    mix += ec * _natural_rows(o2, so2, dils[2])
    mix = mix / (ea + eb + ec)
    out_ref[...] = h_ref[...] + jnp.dot(mix.astype(BF16), w_ref[...], preferred_element_type=F32)


def _merge_out_proj(outs, lses, w, h, seq):
    t = h.shape[0]
    tiles = seq // ROW_TILE
    slabs = [pl.BlockSpec((None, ROW_TILE // d, d * ATTN_WIDTH), lambda i: (i // tiles, i % tiles, 0))
             for _, d in ATTN_PATTERNS]
    full = pl.BlockSpec((ROW_TILE, D_MODEL), lambda i: (i, 0))
    stage = pltpu.VMEM((ATTN_WIDTH // LANES, ROW_TILE, LANES), F32)
    return pl.pallas_call(
        _merge_kernel,
        grid=(t // ROW_TILE,),
        in_specs=slabs + slabs + [pl.BlockSpec((ATTN_WIDTH, D_MODEL), lambda i: (0, 0)), full],
        out_specs=full,
        out_shape=jax.ShapeDtypeStruct((t, D_MODEL), F32),
        scratch_shapes=[stage, stage, stage, stage],
        compiler_params=_params(("parallel",)),
        name="attn_merge_out",
    )(*outs, *lses, w, h)


def _router_kernel(x_ref, g_ref, whi_ref, wlo_ref, y_ref, lg_ref):
    y = _rmsnorm_rows(x_ref[...], g_ref[...])
    y_hi = y.astype(BF16)
    y_lo = (y - y_hi.astype(F32)).astype(BF16)
    whi = whi_ref[...]
    lg = jnp.dot(y_hi, whi, preferred_element_type=F32)
    lg += jnp.dot(y_lo, whi, preferred_element_type=F32)
    lg += jnp.dot(y_hi, wlo_ref[...], preferred_element_type=F32)
    y_ref[...] = y_hi
    lg_ref[...] = lg


def _router_logits(h, gain, wcat):
    t = h.shape[0]
    whi = wcat.astype(BF16)
    wlo = (wcat - whi.astype(F32)).astype(BF16)
    wspec = pl.BlockSpec((D_MODEL, LANES), lambda i: (0, 0))
    return pl.pallas_call(
        _router_kernel,
        grid=(t // ROW_TILE,),
        in_specs=[pl.BlockSpec((ROW_TILE, D_MODEL), lambda i: (i, 0)),
                  pl.BlockSpec((1, D_MODEL), lambda i: (0, 0)), wspec, wspec],
        out_specs=[pl.BlockSpec((ROW_TILE, D_MODEL), lambda i: (i, 0)),
                   pl.BlockSpec((ROW_TILE, LANES), lambda i: (i, 0))],
        out_shape=[jax.ShapeDtypeStruct((t, D_MODEL), BF16),
                   jax.ShapeDtypeStruct((t, LANES), F32)],
        compiler_params=_params(("parallel",)),
        name="router_logits",
    )(h, gain.reshape(1, D_MODEL), whi, wlo)


def _silu(a):
    return a / (1.0 + jnp.exp(-a))


def _moe_kernel(ea_ref, eb_ref, x_ref, g_ref, w1a, w3a, w2a, w1b, w3b, w2b, o_ref):
    del ea_ref, eb_ref
    x = x_ref[...]
    g = g_ref[...]
    ha = _silu(jnp.dot(x, w1a[...], preferred_element_type=F32))
    ha = ha * jnp.dot(x, w3a[...], preferred_element_type=F32) * g[:, 0:1]
    hb = _silu(jnp.dot(x, w1b[...], preferred_element_type=F32))
    hb = hb * jnp.dot(x, w3b[...], preferred_element_type=F32) * g[:, 1:2]
    out = jnp.dot(ha.astype(BF16), w2a[...], preferred_element_type=F32)
    out += jnp.dot(hb.astype(BF16), w2b[...], preferred_element_type=F32)
    o_ref[...] = out


def _expert_blocks(xs, gates, block_ea, block_eb, w1, w3, w2):
    rows = xs.shape[0]
    n_blocks = rows // MOE_ROWS
    up = (None, D_MODEL, D_EXPERT)
    down = (None, D_EXPERT, D_MODEL)
    grid_spec = pltpu.PrefetchScalarGridSpec(
        num_scalar_prefetch=2,
        grid=(n_blocks,),
        in_specs=[pl.BlockSpec((MOE_ROWS, D_MODEL), lambda i, ea, eb: (i, 0)),
                  pl.BlockSpec((MOE_ROWS, 2), lambda i, ea, eb: (i, 0)),
                  pl.BlockSpec(up, lambda i, ea, eb: (ea[i], 0, 0)),
                  pl.BlockSpec(up, lambda i, ea, eb: (ea[i], 0, 0)),
                  pl.BlockSpec(down, lambda i, ea, eb: (ea[i], 0, 0)),
                  pl.BlockSpec(up, lambda i, ea, eb: (eb[i], 0, 0)),
                  pl.BlockSpec(up, lambda i, ea, eb: (eb[i], 0, 0)),
                  pl.BlockSpec(down, lambda i, ea, eb: (eb[i], 0, 0))],
        out_specs=pl.BlockSpec((MOE_ROWS, D_MODEL), lambda i, ea, eb: (i, 0)),
    )
    return pl.pallas_call(
        _moe_kernel,
        grid_spec=grid_spec,
        out_shape=jax.ShapeDtypeStruct((rows, D_MODEL), F32),
        compiler_params=_params(("arbitrary",)),
        name="expert_pair_blocks",
    )(block_ea, block_eb, xs, gates, w1, w3, w2, w1, w3, w2)


def _route(logits, bg, be):
    t = logits.shape[0]
    lg = logits[:, :N_EXPERT_GROUPS] + bg.astype(F32)
    pg = jax.nn.softmax(lg, axis=-1)
    g_star = jnp.argmax(pg, axis=-1).astype(jnp.int32)
    gate1 = jnp.take_along_axis(pg, g_star[:, None], axis=1)[:, 0]
    le = logits[:, N_EXPERT_GROUPS:N_EXPERT_GROUPS + N_EXPERTS].reshape(t, N_EXPERT_GROUPS, EXPERTS_PER_GROUP)
    le = le + be.astype(F32)
    le = jnp.take_along_axis(le, g_star[:, None, None], axis=1)[:, 0]
    top_vals, top_idx = lax.top_k(le, 2)
    gates = gate1[:, None] * jax.nn.softmax(top_vals, axis=-1)
    first_low = top_idx[:, 0] < top_idx[:, 1]
    lo = jnp.minimum(top_idx[:, 0], top_idx[:, 1]).astype(jnp.int32)
    hi = jnp.maximum(top_idx[:, 0], top_idx[:, 1]).astype(jnp.int32)
    g_lo = jnp.where(first_low, gates[:, 0], gates[:, 1])
    g_hi = jnp.where(first_low, gates[:, 1], gates[:, 0])
    pair = lo * (2 * EXPERTS_PER_GROUP - 1 - lo) // 2 + (hi - lo - 1)
    cls = g_star * len(PAIRS) + pair
    return cls, jnp.stack([g_lo, g_hi], axis=1)


def _hierarchical_moe(h, gain, wg, bg, we, be, w1, w3, w2):
    t = h.shape[0]
    wcat = jnp.concatenate(
        [wg, we.transpose(1, 0, 2).reshape(D_MODEL, N_EXPERTS),
         jnp.zeros((D_MODEL, LANES - N_EXPERT_GROUPS - N_EXPERTS), F32)], axis=1)
    y, logits = _router_logits(h, gain, wcat)
    cls, gates = _route(logits, bg, be)

    onehot = (cls[:, None] == jnp.arange(N_CLASSES, dtype=jnp.int32)[None, :]).astype(jnp.int32)
    rank = jnp.take_along_axis(jnp.cumsum(onehot, axis=0), cls[:, None], axis=1)[:, 0] - 1
    counts = jnp.sum(onehot, axis=0)
    padded = ((counts + MOE_ROWS - 1) // MOE_ROWS) * MOE_ROWS
    pends = jnp.cumsum(padded)
    dest = (pends - padded)[cls] + rank
    n_blocks = -(-(t + N_CLASSES * (MOE_ROWS - 1)) // MOE_ROWS)
    rows = n_blocks * MOE_ROWS
    row_tok = jnp.zeros((rows,), jnp.int32).at[dest].set(jnp.arange(t, dtype=jnp.int32))
    block_cls = jnp.minimum(
        jnp.searchsorted(pends, jnp.arange(n_blocks, dtype=jnp.int32) * MOE_ROWS, side="right"),
        N_CLASSES - 1).astype(jnp.int32)
    pair_lo = jnp.asarray([p[0] for p in PAIRS], jnp.int32)
    pair_hi = jnp.asarray([p[1] for p in PAIRS], jnp.int32)
    grp = block_cls // len(PAIRS)
    block_ea = grp * EXPERTS_PER_GROUP + pair_lo[block_cls % len(PAIRS)]
    block_eb = grp * EXPERTS_PER_GROUP + pair_hi[block_cls % len(PAIRS)]

    xs = jnp.take(y, row_tok, axis=0)
    row_gates = jnp.take(gates, row_tok, axis=0)
    ys = _expert_blocks(xs, row_gates, block_ea, block_eb, w1, w3, w2)
    return h + jnp.take(ys, dest, axis=0)


def _norm_proj_kernel(x_ref, g_ref, w_ref, o_ref):
    y = _rmsnorm_rows(x_ref[...], g_ref[...]).astype(BF16)
    o_ref[...] = jnp.dot(y, w_ref[...], preferred_element_type=F32)


def _norm_proj(x, gain, w):
    t, n = x.shape[0], w.shape[1]
    return pl.pallas_call(
        _norm_proj_kernel,
        grid=(t // ROW_TILE,),
        in_specs=[pl.BlockSpec((ROW_TILE, D_MODEL), lambda i: (i, 0)),
                  pl.BlockSpec((1, D_MODEL), lambda i: (0, 0)),
                  pl.BlockSpec((D_MODEL, n), lambda i: (0, 0))],
        out_specs=pl.BlockSpec((ROW_TILE, n), lambda i: (i, 0)),
        out_shape=jax.ShapeDtypeStruct((t, n), F32),
        compiler_params=_params(("parallel",)),
        name="norm_proj",
    )(x, gain.reshape(1, D_MODEL), w)


def _proj_residual_kernel(z_ref, w_ref, h_ref, o_ref):
    o_ref[...] = h_ref[...] + jnp.dot(z_ref[...], w_ref[...], preferred_element_type=F32)


def _proj_residual(z, w, h):
    t, n = h.shape
    k = z.shape[1]
    return pl.pallas_call(
        _proj_residual_kernel,
        grid=(t // ROW_TILE,),
        in_specs=[pl.BlockSpec((ROW_TILE, k), lambda i: (i, 0)),
                  pl.BlockSpec((k, n), lambda i: (0, 0)),
                  pl.BlockSpec((ROW_TILE, n), lambda i: (i, 0))],
        out_specs=pl.BlockSpec((ROW_TILE, n), lambda i: (i, 0)),
        out_shape=jax.ShapeDtypeStruct((t, n), F32),
        compiler_params=_params(("parallel",)),
        name="proj_residual",
    )(z, w, h)


def _pool_kernel(u_ref, wg_ref, sc_ref, z_ref):
    seq = u_ref.shape[0]
    row = lax.broadcasted_iota(jnp.int32, (seq, POOL_GROUP_DIM), 0)
    for g, window in enumerate(POOL_WINDOWS):
        cols = slice(g * POOL_GROUP_DIM, (g + 1) * POOL_GROUP_DIM)
        u = u_ref[:, cols]
        s = u
        step = 1
        while step < window:
            s = s + jnp.where(row >= step, pltpu.roll(s, step, 0), 0.0)
            step *= 2
        count = jnp.minimum(row + 1, window).astype(F32)
        pooled = (s / count - u).astype(BF16)
        z = jnp.dot(pooled, wg_ref[g], preferred_element_type=F32) * sc_ref[:, cols]
        z_ref[:, cols] = z.astype(BF16)


def _pool_mixer_inner(u, w_group, scale, batch, seq):
    blk = pl.BlockSpec((None, seq, D_MODEL), lambda b: (b, 0, 0))
    z = pl.pallas_call(
        _pool_kernel,
        grid=(batch,),
        in_specs=[blk,
                  pl.BlockSpec((N_POOL_GROUPS, POOL_GROUP_DIM, POOL_GROUP_DIM), lambda b: (0, 0, 0)),
                  pl.BlockSpec((1, D_MODEL), lambda b: (0, 0))],
        out_specs=blk,
        out_shape=jax.ShapeDtypeStruct((batch, seq, D_MODEL), BF16),
        compiler_params=_params(("parallel",)),
        name="causal_pool",
    )(u.reshape(batch, seq, D_MODEL), w_group, scale.reshape(1, D_MODEL))
    return z.reshape(batch * seq, D_MODEL)


def _final_norm_kernel(x_ref, g_ref, o_ref):
    o_ref[...] = _rmsnorm_rows(x_ref[...], g_ref[...])


def _final_norm(h, gain):
    t = h.shape[0]
    spec = pl.BlockSpec((ROW_TILE, D_MODEL), lambda i: (i, 0))
    return pl.pallas_call(
        _final_norm_kernel,
        grid=(t // ROW_TILE,),
        in_specs=[spec, pl.BlockSpec((1, D_MODEL), lambda i: (0, 0))],
        out_specs=spec,
        out_shape=jax.ShapeDtypeStruct((t, D_MODEL), F32),
        compiler_params=_params(("parallel",)),
        name="final_norm",
    )(h, gain.reshape(1, D_MODEL))


def kernel(x, positions, norm_mix, norm_ffn, norm_final, attn_w_in, attn_w_out, pool_w_in, pool_w_group, pool_scale, pool_w_out, router_group_w, router_group_b, router_expert_w, router_expert_b, expert_w1, expert_w3, expert_w2):
    batch, seq, _ = x.shape
    t = batch * seq
    h = x.reshape(t, D_MODEL)
    w1 = expert_w1.astype(BF16)
    w3 = expert_w3.astype(BF16)
    w2 = expert_w2.astype(BF16)

    c, s1, s2 = _rope_tables(positions)
    qkv = _qkv_proj(h, norm_mix[0], attn_w_in[0].astype(BF16), c, s1, s2, batch, seq)
    outs, lses = [], []
    for g in range(N_ATTN_GROUPS):
        o, lse = _dilated_attention(qkv[3 * g], qkv[3 * g + 1], qkv[3 * g + 2], g, batch, seq)
        outs.append(o)
        lses.append(lse)
    h = _merge_out_proj(outs, lses, attn_w_out[0].astype(BF16), h, seq)
    h = _hierarchical_moe(h, norm_ffn[0], router_group_w[0], router_group_b[0],
                          router_expert_w[0], router_expert_b[0], w1[0], w3[0], w2[0])

    u = _norm_proj(h, norm_mix[1], pool_w_in[0].astype(BF16))
    z = _pool_mixer_inner(u, pool_w_group[0].astype(BF16), pool_scale[0], batch, seq)
    h = _proj_residual(z, pool_w_out[0].astype(BF16), h)
    h = _hierarchical_moe(h, norm_ffn[1], router_group_w[1], router_group_b[1],
                          router_expert_w[1], router_expert_b[1], w1[1], w3[1], w2[1])

    return _final_norm(h, norm_final).reshape(batch, seq, D_MODEL)
```

```python
import functools
import math

import jax
import jax.numpy as jnp
from jax import lax
from jax.experimental import pallas as pl
from jax.experimental.pallas import tpu as pltpu

F32 = jnp.float32
BF16 = jnp.bfloat16

D_MODEL = 1024
ATTN_PATTERNS = ((128, 1), (512, 4), (2048, 16))
N_ATTN_GROUPS = len(ATTN_PATTERNS)
HEADS = 8
HEAD_DIM = 64
ATTN_WIDTH = HEADS * HEAD_DIM
QKV_WIDTH = N_ATTN_GROUPS * 3 * ATTN_WIDTH
ROT_DIM = HEAD_DIM // 4
ROPE_THETA = 500000.0
ATTN_BLOCK = 128
POOL_WINDOWS = (2, 4, 8, 16)
N_POOL_GROUPS = len(POOL_WINDOWS)
POOL_GROUP_DIM = D_MODEL // N_POOL_GROUPS
N_EXPERT_GROUPS = 4
EXPERTS_PER_GROUP = 4
N_EXPERTS = N_EXPERT_GROUPS * EXPERTS_PER_GROUP
D_EXPERT = D_MODEL // 2
RMS_EPS = 1e-6

PAIRS = ((0, 1), (0, 2), (0, 3), (1, 2), (1, 3), (2, 3))
N_CLASSES = N_EXPERT_GROUPS * len(PAIRS)

ROW_TILE = 512
COL_TILE = 512
MOE_ROWS = 256
LANES = 128
VMEM_LIMIT = 48 * 1024 * 1024
MASKED = -1e30


def _params(sem):
    return pltpu.CompilerParams(dimension_semantics=sem, vmem_limit_bytes=VMEM_LIMIT)


def _rmsnorm_rows(x, g):
    ms = jnp.mean(x * x, axis=-1, keepdims=True)
    return (x * lax.rsqrt(ms + RMS_EPS)) * g


def _rope_kernel(pos_ref, freq_ref, c_ref, s1_ref, s2_ref):
    ang = pos_ref[...].astype(F32) * freq_ref[...]
    lane = lax.broadcasted_iota(jnp.int32, ang.shape, 1) % HEAD_DIM
    sin = jnp.sin(ang)
    c_ref[...] = jnp.cos(ang)
    s1_ref[...] = jnp.where(lane < ROT_DIM // 2, -sin, 0.0)
    s2_ref[...] = jnp.where(lane >= ROT_DIM // 2, sin, 0.0)


def _rope_tables(positions):
    t = positions.size
    half = ROT_DIM // 2
    inv = [ROPE_THETA ** (-(i * 2.0 / ROT_DIM)) for i in range(half)]
    per_head = inv + inv + [0.0] * (HEAD_DIM - ROT_DIM)
    freq = jnp.asarray([per_head * (LANES // HEAD_DIM)], F32)
    tm = 1024
    out = jax.ShapeDtypeStruct((t, LANES), F32)
    spec = pl.BlockSpec((tm, LANES), lambda i: (i, 0))
    return pl.pallas_call(
        _rope_kernel,
        grid=(t // tm,),
        in_specs=[pl.BlockSpec((tm, 1), lambda i: (i, 0)),
                  pl.BlockSpec((1, LANES), lambda i: (0, 0))],
        out_specs=[spec, spec, spec],
        out_shape=[out, out, out],
        compiler_params=_params(("parallel",)),
        name="rope_tables",
    )(positions.reshape(t, 1), freq)


def _qkv_kernel(x_ref, g_ref, w_ref, c_ref, s1_ref, s2_ref, *rest):
    o_refs, (y_sc, acc_sc) = rest[:-2], rest[-2:]
    y_sc[...] = _rmsnorm_rows(x_ref[...], g_ref[...]).astype(BF16)
    chunks = COL_TILE // LANES
    for col in range(QKV_WIDTH // COL_TILE):
        dil = ATTN_PATTERNS[col // 3][1]
        n = ROW_TILE // dil
        buf = col % 2
        acc = jnp.dot(y_sc[...], w_ref[:, col * COL_TILE:(col + 1) * COL_TILE],
                      preferred_element_type=F32)
        for c in range(chunks):
            acc_sc[buf, c] = acc[:, c * LANES:(c + 1) * LANES]
        rotary = col % 3 != 2
        for p in range(dil):
            rows = pl.ds(p, n, stride=dil) if dil > 1 else slice(None)
            if rotary:
                cos, sin_up, sin_dn = c_ref[rows, :], s1_ref[rows, :], s2_ref[rows, :]
            for c in range(chunks):
                a = acc_sc[buf, c, rows, :]
                if rotary:
                    a = (a * cos + pltpu.roll(a, LANES - ROT_DIM // 2, 1) * sin_up
                         + pltpu.roll(a, ROT_DIM // 2, 1) * sin_dn)
                lanes = slice(p * COL_TILE + c * LANES, p * COL_TILE + (c + 1) * LANES)
                o_refs[col][:, lanes] = a.astype(BF16)


def _qkv_proj(x, gain, w, c, s1, s2, batch, seq):
    t = x.shape[0]
    tiles = seq // ROW_TILE
    tab = pl.BlockSpec((ROW_TILE, LANES), lambda i: (i, 0))
    out_specs, out_shapes = [], []
    for _, dil in ATTN_PATTERNS:
        for _ in range(3):
            out_specs.append(pl.BlockSpec((None, ROW_TILE // dil, dil * ATTN_WIDTH),
                                          lambda i: (i // tiles, i % tiles, 0)))
            out_shapes.append(jax.ShapeDtypeStruct((batch, seq // dil, dil * ATTN_WIDTH), BF16))
    return pl.pallas_call(
        _qkv_kernel,
        grid=(t // ROW_TILE,),
        in_specs=[pl.BlockSpec((ROW_TILE, D_MODEL), lambda i: (i, 0)),
                  pl.BlockSpec((1, D_MODEL), lambda i: (0, 0)),
                  pl.BlockSpec((D_MODEL, QKV_WIDTH), lambda i: (0, 0)),
                  tab, tab, tab],
        out_specs=out_specs,
        out_shape=out_shapes,
        scratch_shapes=[pltpu.VMEM((ROW_TILE, D_MODEL), BF16),
                        pltpu.VMEM((2, COL_TILE // LANES, ROW_TILE, LANES), F32)],
        compiler_params=_params(("parallel",)),
        name="qkv_proj",
    )(x, gain.reshape(1, D_MODEL), w, c, s1, s2)


def _attn_kernel(q_ref, kp_ref, kc_ref, vp_ref, vc_ref, o_ref, l_ref):
    j = pl.program_id(2)
    blk = ATTN_BLOCK
    row = lax.broadcasted_iota(jnp.int32, (blk, 2 * blk), 0)
    col = lax.broadcasted_iota(jnp.int32, (blk, 2 * blk), 1)
    lo = jnp.maximum(row, jnp.where(j == 0, blk, 0))
    bias = jnp.where(col >= lo, jnp.where(col <= row + blk, 0.0, MASKED), MASKED)
    scale = 1.0 / math.sqrt(HEAD_DIM)
    outs, lses = [], []
    for h in range(HEADS):
        sl = slice(h * HEAD_DIM, (h + 1) * HEAD_DIM)
        q = q_ref[:, sl]
        k = jnp.concatenate([kp_ref[:, sl], kc_ref[:, sl]], axis=0)
        v = jnp.concatenate([vp_ref[:, sl], vc_ref[:, sl]], axis=0)
        s = lax.dot_general(q, k, (((1,), (1,)), ((), ())), preferred_element_type=F32)
        s = s * scale + bias
        m = jnp.max(s, axis=-1, keepdims=True)
        p = jnp.exp(s - m)
        den = jnp.sum(p, axis=-1, keepdims=True)
        o = jnp.dot(p.astype(BF16), v, preferred_element_type=F32) / den
        outs.append(o)
        lses.append(jnp.broadcast_to(m + jnp.log(den), (blk, HEAD_DIM)))
    o_ref[...] = jnp.concatenate(outs, axis=1).astype(BF16)
    l_ref[...] = jnp.concatenate(lses, axis=1)


def _dilated_attention(q, k, v, group, batch, seq):
    _, dil = ATTN_PATTERNS[group]
    length = seq // dil
    nb = length // ATTN_BLOCK
    cur = pl.BlockSpec((None, ATTN_BLOCK, ATTN_WIDTH), lambda b, p, j: (b, j, p))
    prev = pl.BlockSpec((None, ATTN_BLOCK, ATTN_WIDTH), lambda b, p, j: (b, jnp.maximum(j - 1, 0), p))
    return pl.pallas_call(
        _attn_kernel,
        grid=(batch, dil, nb),
        in_specs=[cur, prev, cur, prev, cur],
        out_specs=[cur, cur],
        out_shape=[jax.ShapeDtypeStruct((batch, length, dil * ATTN_WIDTH), BF16),
                   jax.ShapeDtypeStruct((batch, length, dil * ATTN_WIDTH), F32)],
        compiler_params=_params(("parallel", "parallel", "arbitrary")),
        name=f"dilated_attn_g{group}",
    )(q, k, k, v, v)


def _natural_rows(ref, stage, dil):
    if dil == 1:
        return ref[...].astype(F32)
    n = ROW_TILE // dil
    chunks = ATTN_WIDTH // LANES
    for p in range(dil):
        for c in range(chunks):
            lanes = slice(p * ATTN_WIDTH + c * LANES, p * ATTN_WIDTH + (c + 1) * LANES)
            stage[c, pl.ds(p, n, stride=dil), :] = ref[:, lanes].astype(F32)
    return jnp.concatenate([stage[c] for c in range(chunks)], axis=1)


def _merge_kernel(o0, o1, o2, l0, l1, l2, w_ref, h_ref, out_ref, so1, so2, sl1, sl2):
    dils = [d for _, d in ATTN_PATTERNS]
    a = _natural_rows(l0, None, dils[0])
    b = _natural_rows(l1, sl1, dils[1])
    c = _natural_rows(l2, sl2, dils[2])
    m = jnp.maximum(jnp.maximum(a, b), c)
    ea, eb, ec = jnp.exp(a - m), jnp.exp(b - m), jnp.exp(c - m)
    mix = ea * _natural_rows(o0, None, dils[0])
    mix += eb * _natural_rows(o1, so1, dils[1])
    mix += ec * _natural_rows(o2, so2, dils[2])
    mix = mix / (ea + eb + ec)
    out_ref[...] = h_ref[...] + jnp.dot(mix.astype(BF16), w_ref[...], preferred_element_type=F32)


def _merge_out_proj(outs, lses, w, h, seq):
    t = h.shape[0]
    tiles = seq // ROW_TILE
    slabs = [pl.BlockSpec((None, ROW_TILE // d, d * ATTN_WIDTH), lambda i: (i // tiles, i % tiles, 0))
             for _, d in ATTN_PATTERNS]
    full = pl.BlockSpec((ROW_TILE, D_MODEL), lambda i: (i, 0))
    stage = pltpu.VMEM((ATTN_WIDTH // LANES, ROW_TILE, LANES), F32)
    return pl.pallas_call(
        _merge_kernel,
        grid=(t // ROW_TILE,),
        in_specs=slabs + slabs + [pl.BlockSpec((ATTN_WIDTH, D_MODEL), lambda i: (0, 0)), full],
        out_specs=full,
        out_shape=jax.ShapeDtypeStruct((t, D_MODEL), F32),
        scratch_shapes=[stage, stage, stage, stage],
        compiler_params=_params(("parallel",)),
        name="attn_merge_out",
    )(*outs, *lses, w, h)


def _router_kernel(x_ref, g_ref, whi_ref, wlo_ref, lg_ref):
    y = _rmsnorm_rows(x_ref[...], g_ref[...])
    y_hi = y.astype(BF16)
    y_lo = (y - y_hi.astype(F32)).astype(BF16)
    whi = whi_ref[...]
    lg = jnp.dot(y_hi, whi, preferred_element_type=F32)
    lg += jnp.dot(y_lo, whi, preferred_element_type=F32)
    lg += jnp.dot(y_hi, wlo_ref[...], preferred_element_type=F32)
    lg_ref[...] = lg


def _router_logits(h, gain, wcat):
    t = h.shape[0]
    whi = wcat.astype(BF16)
    wlo = (wcat - whi.astype(F32)).astype(BF16)
    wspec = pl.BlockSpec((D_MODEL, LANES), lambda i: (0, 0))
    return pl.pallas_call(
        _router_kernel,
        grid=(t // ROW_TILE,),
        in_specs=[pl.BlockSpec((ROW_TILE, D_MODEL), lambda i: (i, 0)),
                  pl.BlockSpec((1, D_MODEL), lambda i: (0, 0)), wspec, wspec],
        out_specs=pl.BlockSpec((ROW_TILE, LANES), lambda i: (i, 0)),
        out_shape=jax.ShapeDtypeStruct((t, LANES), F32),
        compiler_params=_params(("parallel",)),
        name="router_logits",
    )(h, gain.reshape(1, D_MODEL), whi, wlo)


ROW_PAYLOAD = D_MODEL + LANES


def _dispatch_kernel(dest_ref, h_ref, g_ref, gate_ref, xs_init, xs_hbm, rowbuf, sem):
    del xs_init
    base = pl.program_id(0) * ROW_TILE
    rowbuf[:, :D_MODEL] = _rmsnorm_rows(h_ref[...], g_ref[...])
    rowbuf[:, D_MODEL:] = gate_ref[...]

    def row_copy(r):
        return pltpu.make_async_copy(rowbuf.at[pl.ds(r, 1), :],
                                     xs_hbm.at[pl.ds(dest_ref[base + r], 1), :], sem)

    def start(r, carry):
        row_copy(r).start()
        return carry

    def wait(r, carry):
        row_copy(r).wait()
        return carry

    lax.fori_loop(0, ROW_TILE, start, 0, unroll=8)
    lax.fori_loop(0, ROW_TILE, wait, 0, unroll=8)


def _dispatch_rows(h, gain, gate_pad, dest, rows):
    t = h.shape[0]
    grid_spec = pltpu.PrefetchScalarGridSpec(
        num_scalar_prefetch=1,
        grid=(t // ROW_TILE,),
        in_specs=[pl.BlockSpec((ROW_TILE, D_MODEL), lambda i, d: (i, 0)),
                  pl.BlockSpec((1, D_MODEL), lambda i, d: (0, 0)),
                  pl.BlockSpec((ROW_TILE, LANES), lambda i, d: (i, 0)),
                  pl.BlockSpec(memory_space=pl.ANY)],
        out_specs=pl.BlockSpec(memory_space=pl.ANY),
        scratch_shapes=[pltpu.VMEM((ROW_TILE, ROW_PAYLOAD), F32), pltpu.SemaphoreType.DMA(())],
    )
    return pl.pallas_call(
        _dispatch_kernel,
        grid_spec=grid_spec,
        out_shape=jax.ShapeDtypeStruct((rows, ROW_PAYLOAD), F32),
        input_output_aliases={4: 0},
        compiler_params=_params(("arbitrary",)),
        name="moe_dispatch",
    )(dest, h, gain.reshape(1, D_MODEL), gate_pad, jnp.zeros((rows, ROW_PAYLOAD), F32))


def _combine_kernel(dest_ref, h_ref, g_ref, ys_hbm, o_ref, buf, sem, *, final_norm):
    base = pl.program_id(0) * ROW_TILE

    def row_copy(r):
        return pltpu.make_async_copy(ys_hbm.at[pl.ds(dest_ref[base + r], 1), :],
                                     buf.at[pl.ds(r, 1), :], sem)

    def start(r, carry):
        row_copy(r).start()
        return carry

    def wait(r, carry):
        row_copy(r).wait()
        return carry

    lax.fori_loop(0, ROW_TILE, start, 0, unroll=8)
    lax.fori_loop(0, ROW_TILE, wait, 0, unroll=8)
    out = h_ref[...] + buf[...]
    o_ref[...] = _rmsnorm_rows(out, g_ref[...]) if final_norm else out


def _combine_rows(h, ys, dest, final_gain):
    t = h.shape[0]
    final_norm = final_gain is not None
    gain = final_gain if final_norm else jnp.ones((D_MODEL,), F32)
    tile = pl.BlockSpec((ROW_TILE, D_MODEL), lambda i, d: (i, 0))
    grid_spec = pltpu.PrefetchScalarGridSpec(
        num_scalar_prefetch=1,
        grid=(t // ROW_TILE,),
        in_specs=[tile, pl.BlockSpec((1, D_MODEL), lambda i, d: (0, 0)),
                  pl.BlockSpec(memory_space=pl.ANY)],
        out_specs=tile,
        scratch_shapes=[pltpu.VMEM((ROW_TILE, D_MODEL), F32), pltpu.SemaphoreType.DMA(())],
    )
    return pl.pallas_call(
        functools.partial(_combine_kernel, final_norm=final_norm),
        grid_spec=grid_spec,
        out_shape=jax.ShapeDtypeStruct((t, D_MODEL), F32),
        compiler_params=_params(("arbitrary",)),
        name="moe_combine",
    )(dest, h, gain.reshape(1, D_MODEL), ys)


def _silu(a):
    return a / (1.0 + jnp.exp(-a))


def _moe_kernel(ea_ref, eb_ref, x_ref, w1a, w3a, w2a, w1b, w3b, w2b, o_ref):
    del ea_ref, eb_ref
    x = x_ref[:, :D_MODEL].astype(BF16)
    g = x_ref[:, D_MODEL:]
    ha = _silu(jnp.dot(x, w1a[...], preferred_element_type=F32))
    ha = ha * jnp.dot(x, w3a[...], preferred_element_type=F32) * g[:, 0:1]
    hb = _silu(jnp.dot(x, w1b[...], preferred_element_type=F32))
    hb = hb * jnp.dot(x, w3b[...], preferred_element_type=F32) * g[:, 1:2]
    out = jnp.dot(ha.astype(BF16), w2a[...], preferred_element_type=F32)
    out += jnp.dot(hb.astype(BF16), w2b[...], preferred_element_type=F32)
    o_ref[...] = out


def _expert_blocks(xs, block_ea, block_eb, w1, w3, w2):
    rows = xs.shape[0]
    n_blocks = rows // MOE_ROWS
    up = (None, D_MODEL, D_EXPERT)
    down = (None, D_EXPERT, D_MODEL)
    grid_spec = pltpu.PrefetchScalarGridSpec(
        num_scalar_prefetch=2,
        grid=(n_blocks,),
        in_specs=[pl.BlockSpec((MOE_ROWS, ROW_PAYLOAD), lambda i, ea, eb: (i, 0)),
                  pl.BlockSpec(up, lambda i, ea, eb: (ea[i], 0, 0)),
                  pl.BlockSpec(up, lambda i, ea, eb: (ea[i], 0, 0)),
                  pl.BlockSpec(down, lambda i, ea, eb: (ea[i], 0, 0)),
                  pl.BlockSpec(up, lambda i, ea, eb: (eb[i], 0, 0)),
                  pl.BlockSpec(up, lambda i, ea, eb: (eb[i], 0, 0)),
                  pl.BlockSpec(down, lambda i, ea, eb: (eb[i], 0, 0))],
        out_specs=pl.BlockSpec((MOE_ROWS, D_MODEL), lambda i, ea, eb: (i, 0)),
    )
    return pl.pallas_call(
        _moe_kernel,
        grid_spec=grid_spec,
        out_shape=jax.ShapeDtypeStruct((rows, D_MODEL), F32),
        compiler_params=_params(("arbitrary",)),
        name="expert_pair_blocks",
    )(block_ea, block_eb, xs, w1, w3, w2, w1, w3, w2)


def _route(logits, bg, be):
    t = logits.shape[0]
    lg = logits[:, :N_EXPERT_GROUPS] + bg.astype(F32)
    pg = jax.nn.softmax(lg, axis=-1)
    g_star = jnp.argmax(pg, axis=-1).astype(jnp.int32)
    gate1 = jnp.take_along_axis(pg, g_star[:, None], axis=1)[:, 0]
    le = logits[:, N_EXPERT_GROUPS:N_EXPERT_GROUPS + N_EXPERTS].reshape(t, N_EXPERT_GROUPS, EXPERTS_PER_GROUP)
    le = le + be.astype(F32)
    le = jnp.take_along_axis(le, g_star[:, None, None], axis=1)[:, 0]
    top_vals, top_idx = lax.top_k(le, 2)
    gates = gate1[:, None] * jax.nn.softmax(top_vals, axis=-1)
    first_low = top_idx[:, 0] < top_idx[:, 1]
    lo = jnp.minimum(top_idx[:, 0], top_idx[:, 1]).astype(jnp.int32)
    hi = jnp.maximum(top_idx[:, 0], top_idx[:, 1]).astype(jnp.int32)
    g_lo = jnp.where(first_low, gates[:, 0], gates[:, 1])
    g_hi = jnp.where(first_low, gates[:, 1], gates[:, 0])
    pair = lo * (2 * EXPERTS_PER_GROUP - 1 - lo) // 2 + (hi - lo - 1)
    cls = g_star * len(PAIRS) + pair
    return cls, jnp.stack([g_lo, g_hi], axis=1)


def _hierarchical_moe(h, gain, wg, bg, we, be, w1, w3, w2, final_gain=None):
    t = h.shape[0]
    wcat = jnp.concatenate(
        [wg, we.transpose(1, 0, 2).reshape(D_MODEL, N_EXPERTS),
         jnp.zeros((D_MODEL, LANES - N_EXPERT_GROUPS - N_EXPERTS), F32)], axis=1)
    logits = _router_logits(h, gain, wcat)
    cls, gates = _route(logits, bg, be)

    onehot = (cls[:, None] == jnp.arange(N_CLASSES, dtype=jnp.int32)[None, :]).astype(jnp.int32)
    rank = jnp.take_along_axis(jnp.cumsum(onehot, axis=0), cls[:, None], axis=1)[:, 0] - 1
    counts = jnp.sum(onehot, axis=0)
    padded = ((counts + MOE_ROWS - 1) // MOE_ROWS) * MOE_ROWS
    pends = jnp.cumsum(padded)
    dest = (pends - padded)[cls] + rank
    n_blocks = -(-(t + N_CLASSES * (MOE_ROWS - 1)) // MOE_ROWS)
    rows = n_blocks * MOE_ROWS
    block_cls = jnp.minimum(
        jnp.searchsorted(pends, jnp.arange(n_blocks, dtype=jnp.int32) * MOE_ROWS, side="right"),
        N_CLASSES - 1).astype(jnp.int32)
    pair_lo = jnp.asarray([p[0] for p in PAIRS], jnp.int32)
    pair_hi = jnp.asarray([p[1] for p in PAIRS], jnp.int32)
    grp = block_cls // len(PAIRS)
    block_ea = grp * EXPERTS_PER_GROUP + pair_lo[block_cls % len(PAIRS)]
    block_eb = grp * EXPERTS_PER_GROUP + pair_hi[block_cls % len(PAIRS)]

    dest = dest.astype(jnp.int32)
    gate_pad = jnp.pad(gates, ((0, 0), (0, LANES - gates.shape[1])))
    xs = _dispatch_rows(h, gain, gate_pad, dest, rows)
    ys = _expert_blocks(xs, block_ea, block_eb, w1, w3, w2)
    return _combine_rows(h, ys, dest, final_gain)


def _norm_proj_kernel(x_ref, g_ref, w_ref, o_ref):
    y = _rmsnorm_rows(x_ref[...], g_ref[...]).astype(BF16)
    o_ref[...] = jnp.dot(y, w_ref[...], preferred_element_type=F32)


def _norm_proj(x, gain, w):
    t, n = x.shape[0], w.shape[1]
    return pl.pallas_call(
        _norm_proj_kernel,
        grid=(t // ROW_TILE,),
        in_specs=[pl.BlockSpec((ROW_TILE, D_MODEL), lambda i: (i, 0)),
                  pl.BlockSpec((1, D_MODEL), lambda i: (0, 0)),
                  pl.BlockSpec((D_MODEL, n), lambda i: (0, 0))],
        out_specs=pl.BlockSpec((ROW_TILE, n), lambda i: (i, 0)),
        out_shape=jax.ShapeDtypeStruct((t, n), F32),
        compiler_params=_params(("parallel",)),
        name="norm_proj",
    )(x, gain.reshape(1, D_MODEL), w)


def _proj_residual_kernel(z_ref, w_ref, h_ref, o_ref):
    o_ref[...] = h_ref[...] + jnp.dot(z_ref[...], w_ref[...], preferred_element_type=F32)


def _proj_residual(z, w, h):
    t, n = h.shape
    k = z.shape[1]
    return pl.pallas_call(
        _proj_residual_kernel,
        grid=(t // ROW_TILE,),
        in_specs=[pl.BlockSpec((ROW_TILE, k), lambda i: (i, 0)),
                  pl.BlockSpec((k, n), lambda i: (0, 0)),
                  pl.BlockSpec((ROW_TILE, n), lambda i: (i, 0))],
        out_specs=pl.BlockSpec((ROW_TILE, n), lambda i: (i, 0)),
        out_shape=jax.ShapeDtypeStruct((t, n), F32),
        compiler_params=_params(("parallel",)),
        name="proj_residual",
    )(z, w, h)


def _pool_kernel(u_ref, wg_ref, sc_ref, z_ref):
    seq = u_ref.shape[0]
    row = lax.broadcasted_iota(jnp.int32, (seq, POOL_GROUP_DIM), 0)
    for g, window in enumerate(POOL_WINDOWS):
        cols = slice(g * POOL_GROUP_DIM, (g + 1) * POOL_GROUP_DIM)
        u = u_ref[:, cols]
        s = u
        step = 1
        while step < window:
            s = s + jnp.where(row >= step, pltpu.roll(s, step, 0), 0.0)
            step *= 2
        count = jnp.minimum(row + 1, window).astype(F32)
        pooled = (s / count - u).astype(BF16)
        z = jnp.dot(pooled, wg_ref[g], preferred_element_type=F32) * sc_ref[:, cols]
        z_ref[:, cols] = z.astype(BF16)


def _pool_mixer_inner(u, w_group, scale, batch, seq):
    blk = pl.BlockSpec((None, seq, D_MODEL), lambda b: (b, 0, 0))
    z = pl.pallas_call(
        _pool_kernel,
        grid=(batch,),
        in_specs=[blk,
                  pl.BlockSpec((N_POOL_GROUPS, POOL_GROUP_DIM, POOL_GROUP_DIM), lambda b: (0, 0, 0)),
                  pl.BlockSpec((1, D_MODEL), lambda b: (0, 0))],
        out_specs=blk,
        out_shape=jax.ShapeDtypeStruct((batch, seq, D_MODEL), BF16),
        compiler_params=_params(("parallel",)),
        name="causal_pool",
    )(u.reshape(batch, seq, D_MODEL), w_group, scale.reshape(1, D_MODEL))
    return z.reshape(batch * seq, D_MODEL)


def _final_norm_kernel(x_ref, g_ref, o_ref):
    o_ref[...] = _rmsnorm_rows(x_ref[...], g_ref[...])


def _final_norm(h, gain):
    t = h.shape[0]
    spec = pl.BlockSpec((ROW_TILE, D_MODEL), lambda i: (i, 0))
    return pl.pallas_call(
        _final_norm_kernel,
        grid=(t // ROW_TILE,),
        in_specs=[spec, pl.BlockSpec((1, D_MODEL), lambda i: (0, 0))],
        out_specs=spec,
        out_shape=jax.ShapeDtypeStruct((t, D_MODEL), F32),
        compiler_params=_params(("parallel",)),
        name="final_norm",
    )(h, gain.reshape(1, D_MODEL))


def kernel(x, positions, norm_mix, norm_ffn, norm_final, attn_w_in, attn_w_out, pool_w_in, pool_w_group, pool_scale, pool_w_out, router_group_w, router_group_b, router_expert_w, router_expert_b, expert_w1, expert_w3, expert_w2):
    batch, seq, _ = x.shape
    t = batch * seq
    h = x.reshape(t, D_MODEL)
    w1 = expert_w1.astype(BF16)
    w3 = expert_w3.astype(BF16)
    w2 = expert_w2.astype(BF16)

    c, s1, s2 = _rope_tables(positions)
    qkv = _qkv_proj(h, norm_mix[0], attn_w_in[0].astype(BF16), c, s1, s2, batch, seq)
    outs, lses = [], []
    for g in range(N_ATTN_GROUPS):
        o, lse = _dilated_attention(qkv[3 * g], qkv[3 * g + 1], qkv[3 * g + 2], g, batch, seq)
        outs.append(o)
        lses.append(lse)
    h = _merge_out_proj(outs, lses, attn_w_out[0].astype(BF16), h, seq)
    h = _hierarchical_moe(h, norm_ffn[0], router_group_w[0], router_group_b[0],
                          router_expert_w[0], router_expert_b[0], w1[0], w3[0], w2[0])

    u = _norm_proj(h, norm_mix[1], pool_w_in[0].astype(BF16))
    z = _pool_mixer_inner(u, pool_w_group[0].astype(BF16), pool_scale[0], batch, seq)
    h = _proj_residual(z, pool_w_out[0].astype(BF16), h)
    out = _hierarchical_moe(h, norm_ffn[1], router_group_w[1], router_group_b[1],
                            router_expert_w[1], router_expert_b[1], w1[1], w3[1], w2[1],
                            final_gain=norm_final)
    return out.reshape(batch, seq, D_MODEL)
```

```python
import functools
import math

import jax
import jax.numpy as jnp
from jax import lax
from jax.experimental import pallas as pl
from jax.experimental.pallas import tpu as pltpu

F32 = jnp.float32
BF16 = jnp.bfloat16

D_MODEL = 1024
ATTN_PATTERNS = ((128, 1), (512, 4), (2048, 16))
N_ATTN_GROUPS = len(ATTN_PATTERNS)
HEADS = 8
HEAD_DIM = 64
ATTN_WIDTH = HEADS * HEAD_DIM
QKV_WIDTH = N_ATTN_GROUPS * 3 * ATTN_WIDTH
ROT_DIM = HEAD_DIM // 4
ROPE_THETA = 500000.0
ATTN_BLOCK = 128
POOL_WINDOWS = (2, 4, 8, 16)
N_POOL_GROUPS = len(POOL_WINDOWS)
POOL_GROUP_DIM = D_MODEL // N_POOL_GROUPS
N_EXPERT_GROUPS = 4
EXPERTS_PER_GROUP = 4
N_EXPERTS = N_EXPERT_GROUPS * EXPERTS_PER_GROUP
D_EXPERT = D_MODEL // 2
RMS_EPS = 1e-6

PAIRS = ((0, 1), (0, 2), (0, 3), (1, 2), (1, 3), (2, 3))
N_CLASSES = N_EXPERT_GROUPS * len(PAIRS)

ROW_TILE = 512
COL_TILE = 512
MOE_ROWS = 256
LANES = 128
VMEM_LIMIT = 48 * 1024 * 1024
MASKED = -1e30


def _params(sem):
    return pltpu.CompilerParams(dimension_semantics=sem, vmem_limit_bytes=VMEM_LIMIT)


def _rmsnorm_rows(x, g):
    ms = jnp.mean(x * x, axis=-1, keepdims=True)
    return (x * lax.rsqrt(ms + RMS_EPS)) * g


def _rope_kernel(pos_ref, freq_ref, c_ref, s1_ref, s2_ref):
    ang = pos_ref[...].astype(F32) * freq_ref[...]
    lane = lax.broadcasted_iota(jnp.int32, ang.shape, 1) % HEAD_DIM
    sin = jnp.sin(ang)
    c_ref[...] = jnp.cos(ang)
    s1_ref[...] = jnp.where(lane < ROT_DIM // 2, -sin, 0.0)
    s2_ref[...] = jnp.where(lane >= ROT_DIM // 2, sin, 0.0)


def _rope_tables(positions):
    t = positions.size
    half = ROT_DIM // 2
    inv = [ROPE_THETA ** (-(i * 2.0 / ROT_DIM)) for i in range(half)]
    per_head = inv + inv + [0.0] * (HEAD_DIM - ROT_DIM)
    freq = jnp.asarray([per_head * (LANES // HEAD_DIM)], F32)
    tm = 1024
    out = jax.ShapeDtypeStruct((t, LANES), F32)
    spec = pl.BlockSpec((tm, LANES), lambda i: (i, 0))
    return pl.pallas_call(
        _rope_kernel,
        grid=(t // tm,),
        in_specs=[pl.BlockSpec((tm, 1), lambda i: (i, 0)),
                  pl.BlockSpec((1, LANES), lambda i: (0, 0))],
        out_specs=[spec, spec, spec],
        out_shape=[out, out, out],
        compiler_params=_params(("parallel",)),
        name="rope_tables",
    )(positions.reshape(t, 1), freq)


def _qkv_kernel(x_ref, g_ref, w_ref, c_ref, s1_ref, s2_ref, *rest):
    o_refs, (y_sc, acc_sc) = rest[:-2], rest[-2:]
    y_sc[...] = _rmsnorm_rows(x_ref[...], g_ref[...]).astype(BF16)
    chunks = COL_TILE // LANES
    for col in range(QKV_WIDTH // COL_TILE):
        dil = ATTN_PATTERNS[col // 3][1]
        n = ROW_TILE // dil
        buf = col % 2
        acc = jnp.dot(y_sc[...], w_ref[:, col * COL_TILE:(col + 1) * COL_TILE],
                      preferred_element_type=F32)
        for c in range(chunks):
            acc_sc[buf, c] = acc[:, c * LANES:(c + 1) * LANES]
        rotary = col % 3 != 2
        for p in range(dil):
            rows = pl.ds(p, n, stride=dil) if dil > 1 else slice(None)
            if rotary:
                cos, sin_up, sin_dn = c_ref[rows, :], s1_ref[rows, :], s2_ref[rows, :]
            for c in range(chunks):
                a = acc_sc[buf, c, rows, :]
                if rotary:
                    a = (a * cos + pltpu.roll(a, LANES - ROT_DIM // 2, 1) * sin_up
                         + pltpu.roll(a, ROT_DIM // 2, 1) * sin_dn)
                if col % 3 == 0:
                    a = a * (1.0 / math.sqrt(HEAD_DIM))
                lanes = slice(p * COL_TILE + c * LANES, p * COL_TILE + (c + 1) * LANES)
                o_refs[col][:, lanes] = a.astype(BF16)


def _qkv_proj(x, gain, w, c, s1, s2, batch, seq):
    t = x.shape[0]
    tiles = seq // ROW_TILE
    tab = pl.BlockSpec((ROW_TILE, LANES), lambda i: (i, 0))
    out_specs, out_shapes = [], []
    for _, dil in ATTN_PATTERNS:
        for _ in range(3):
            out_specs.append(pl.BlockSpec((None, ROW_TILE // dil, dil * ATTN_WIDTH),
                                          lambda i: (i // tiles, i % tiles, 0)))
            out_shapes.append(jax.ShapeDtypeStruct((batch, seq // dil, dil * ATTN_WIDTH), BF16))
    return pl.pallas_call(
        _qkv_kernel,
        grid=(t // ROW_TILE,),
        in_specs=[pl.BlockSpec((ROW_TILE, D_MODEL), lambda i: (i, 0)),
                  pl.BlockSpec((1, D_MODEL), lambda i: (0, 0)),
                  pl.BlockSpec((D_MODEL, QKV_WIDTH), lambda i: (0, 0)),
                  tab, tab, tab],
        out_specs=out_specs,
        out_shape=out_shapes,
        scratch_shapes=[pltpu.VMEM((ROW_TILE, D_MODEL), BF16),
                        pltpu.VMEM((2, COL_TILE // LANES, ROW_TILE, LANES), F32)],
        compiler_params=_params(("parallel",)),
        name="qkv_proj",
    )(x, gain.reshape(1, D_MODEL), w, c, s1, s2)


def _attn_kernel(q_ref, kp_ref, kc_ref, vp_ref, vc_ref, o_ref, l_ref):
    j = pl.program_id(2)
    blk = ATTN_BLOCK
    row = lax.broadcasted_iota(jnp.int32, (blk, 2 * blk), 0)
    col = lax.broadcasted_iota(jnp.int32, (blk, 2 * blk), 1)
    lo = jnp.maximum(row, jnp.where(j == 0, blk, 0))
    bias = jnp.where(col >= lo, jnp.where(col <= row + blk, 0.0, MASKED), MASKED)
    outs, lses = [], []
    for h in range(HEADS):
        sl = slice(h * HEAD_DIM, (h + 1) * HEAD_DIM)
        q = q_ref[:, sl]
        k = jnp.concatenate([kp_ref[:, sl], kc_ref[:, sl]], axis=0)
        v = jnp.concatenate([vp_ref[:, sl], vc_ref[:, sl]], axis=0)
        s = lax.dot_general(q, k, (((1,), (1,)), ((), ())), preferred_element_type=F32) + bias
        m = jnp.max(s, axis=-1, keepdims=True)
        p = jnp.exp(s - m)
        den = jnp.sum(p, axis=-1, keepdims=True)
        o = jnp.dot(p.astype(BF16), v, preferred_element_type=F32) / den
        outs.append(o)
        lses.append(jnp.broadcast_to(m + jnp.log(den), (blk, HEAD_DIM)))
    o_ref[...] = jnp.concatenate(outs, axis=1).astype(BF16)
    l_ref[...] = jnp.concatenate(lses, axis=1)


def _dilated_attention(q, k, v, group, batch, seq):
    _, dil = ATTN_PATTERNS[group]
    length = seq // dil
    nb = length // ATTN_BLOCK
    cur = pl.BlockSpec((None, ATTN_BLOCK, ATTN_WIDTH), lambda b, p, j: (b, j, p))
    prev = pl.BlockSpec((None, ATTN_BLOCK, ATTN_WIDTH), lambda b, p, j: (b, jnp.maximum(j - 1, 0), p))
    return pl.pallas_call(
        _attn_kernel,
        grid=(batch, dil, nb),
        in_specs=[cur, prev, cur, prev, cur],
        out_specs=[cur, cur],
        out_shape=[jax.ShapeDtypeStruct((batch, length, dil * ATTN_WIDTH), BF16),
                   jax.ShapeDtypeStruct((batch, length, dil * ATTN_WIDTH), F32)],
        compiler_params=_params(("parallel", "parallel", "arbitrary")),
        name=f"dilated_attn_g{group}",
    )(q, k, k, v, v)


def _natural_rows(ref, stage, dil):
    if dil == 1:
        return ref[...].astype(F32)
    n = ROW_TILE // dil
    chunks = ATTN_WIDTH // LANES
    for p in range(dil):
        for c in range(chunks):
            lanes = slice(p * ATTN_WIDTH + c * LANES, p * ATTN_WIDTH + (c + 1) * LANES)
            stage[c, pl.ds(p, n, stride=dil), :] = ref[:, lanes].astype(F32)
    return jnp.concatenate([stage[c] for c in range(chunks)], axis=1)


def _merge_kernel(o0, o1, o2, l0, l1, l2, w_ref, h_ref, out_ref, so1, so2, sl1, sl2):
    dils = [d for _, d in ATTN_PATTERNS]
    a = _natural_rows(l0, None, dils[0])
    b = _natural_rows(l1, sl1, dils[1])
    c = _natural_rows(l2, sl2, dils[2])
    m = jnp.maximum(jnp.maximum(a, b), c)
    ea, eb, ec = jnp.exp(a - m), jnp.exp(b - m), jnp.exp(c - m)
    mix = ea * _natural_rows(o0, None, dils[0])
    mix += eb * _natural_rows(o1, so1, dils[1])
    mix += ec * _natural_rows(o2, so2, dils[2])
    mix = mix / (ea + eb + ec)
    out_ref[...] = h_ref[...] + jnp.dot(mix.astype(BF16), w_ref[...], preferred_element_type=F32)


def _merge_out_proj(outs, lses, w, h, seq):
    t = h.shape[0]
    tiles = seq // ROW_TILE
    slabs = [pl.BlockSpec((None, ROW_TILE // d, d * ATTN_WIDTH), lambda i: (i // tiles, i % tiles, 0))
             for _, d in ATTN_PATTERNS]
    full = pl.BlockSpec((ROW_TILE, D_MODEL), lambda i: (i, 0))
    stage = pltpu.VMEM((ATTN_WIDTH // LANES, ROW_TILE, LANES), F32)
    return pl.pallas_call(
        _merge_kernel,
        grid=(t // ROW_TILE,),
        in_specs=slabs + slabs + [pl.BlockSpec((ATTN_WIDTH, D_MODEL), lambda i: (0, 0)), full],
        out_specs=full,
        out_shape=jax.ShapeDtypeStruct((t, D_MODEL), F32),
        scratch_shapes=[stage, stage, stage, stage],
        compiler_params=_params(("parallel",)),
        name="attn_merge_out",
    )(*outs, *lses, w, h)


CLASS_ROWS = 32


def _router_kernel(x_ref, g_ref, whi_ref, wlo_ref, b_ref, tri_ref,
                   cls_ref, rank_ref, gate_ref, count_ref, run_sc):
    @pl.when(pl.program_id(0) == 0)
    def _():
        run_sc[...] = jnp.zeros_like(run_sc)

    y = _rmsnorm_rows(x_ref[...], g_ref[...])
    y_hi = y.astype(BF16)
    y_lo = (y - y_hi.astype(F32)).astype(BF16)
    nt = (((1,), (1,)), ((), ()))
    whi = whi_ref[...]
    lg = lax.dot_general(whi, y_hi, nt, preferred_element_type=F32)
    lg += lax.dot_general(whi, y_lo, nt, preferred_element_type=F32)
    lg += lax.dot_general(wlo_ref[...], y_hi, nt, preferred_element_type=F32)
    lg = lg + b_ref[...]
    row = [lg[r:r + 1, :] for r in range(N_EXPERT_GROUPS + N_EXPERTS)]

    best, grp = row[0], jnp.zeros(row[0].shape, jnp.int32)
    for g in range(1, N_EXPERT_GROUPS):
        better = row[g] > best
        grp = jnp.where(better, g, grp)
        best = jnp.where(better, row[g], best)
    gate1 = 1.0 / sum(jnp.exp(row[g] - best) for g in range(N_EXPERT_GROUPS))

    le = []
    for e in range(EXPERTS_PER_GROUP):
        v = row[N_EXPERT_GROUPS + e]
        for g in range(1, N_EXPERT_GROUPS):
            v = jnp.where(grp == g, row[N_EXPERT_GROUPS + g * EXPERTS_PER_GROUP + e], v)
        le.append(v)
    v1, i1 = le[0], jnp.zeros(grp.shape, jnp.int32)
    for e in range(1, EXPERTS_PER_GROUP):
        better = le[e] > v1
        i1 = jnp.where(better, e, i1)
        v1 = jnp.where(better, le[e], v1)
    v2, i2 = jnp.full(v1.shape, -jnp.inf, F32), jnp.zeros(grp.shape, jnp.int32)
    for e in range(EXPERTS_PER_GROUP):
        cand = jnp.where(i1 == e, -jnp.inf, le[e])
        better = cand > v2
        i2 = jnp.where(better, e, i2)
        v2 = jnp.where(better, cand, v2)
    ratio = jnp.exp(v2 - v1)
    g_first = gate1 / (1.0 + ratio)
    g_second = gate1 * ratio / (1.0 + ratio)
    first_low = i1 < i2
    lo = jnp.where(first_low, i1, i2)
    hi = jnp.where(first_low, i2, i1)
    pair = jnp.right_shift(lo * (2 * EXPERTS_PER_GROUP - 1 - lo), 1) + (hi - lo - 1)
    cls = grp * len(PAIRS) + pair
    cls_ref[...] = cls
    gate_ref[0:1, :] = jnp.where(first_low, g_first, g_second)
    gate_ref[1:2, :] = jnp.where(first_low, g_second, g_first)

    onehot = lax.broadcasted_iota(jnp.int32, (CLASS_ROWS, ROW_TILE), 0) == cls
    oh = jnp.where(onehot, 1.0, 0.0)
    earlier = jnp.dot(oh.astype(BF16), tri_ref[...], preferred_element_type=F32)
    running = run_sc[:, 0:1]
    rank_ref[...] = jnp.sum(oh * (earlier + running), axis=0, keepdims=True).astype(jnp.int32)
    run_sc[...] = run_sc[...] + jnp.sum(oh, axis=1, keepdims=True)
    count_ref[...] = run_sc[...]


def _route_tokens(h, gain, wcat, bias):
    t = h.shape[0]
    wt = wcat.T
    whi = wt.astype(BF16)
    wlo = (wt - whi.astype(F32)).astype(BF16)
    idx = jnp.arange(ROW_TILE, dtype=jnp.int32)
    tri = (idx[:, None] < idx[None, :]).astype(BF16)
    wspec = pl.BlockSpec((LANES, D_MODEL), lambda i: (0, 0))
    row_i = pl.BlockSpec((1, ROW_TILE), lambda i: (0, i))
    return pl.pallas_call(
        _router_kernel,
        grid=(t // ROW_TILE,),
        in_specs=[pl.BlockSpec((ROW_TILE, D_MODEL), lambda i: (i, 0)),
                  pl.BlockSpec((1, D_MODEL), lambda i: (0, 0)), wspec, wspec,
                  pl.BlockSpec((LANES, 1), lambda i: (0, 0)),
                  pl.BlockSpec((ROW_TILE, ROW_TILE), lambda i: (0, 0))],
        out_specs=[row_i, row_i, pl.BlockSpec((2, ROW_TILE), lambda i: (0, i)),
                   pl.BlockSpec((CLASS_ROWS, LANES), lambda i: (0, 0))],
        out_shape=[jax.ShapeDtypeStruct((1, t), jnp.int32),
                   jax.ShapeDtypeStruct((1, t), jnp.int32),
                   jax.ShapeDtypeStruct((2, t), F32),
                   jax.ShapeDtypeStruct((CLASS_ROWS, LANES), F32)],
        scratch_shapes=[pltpu.VMEM((CLASS_ROWS, LANES), F32)],
        compiler_params=_params(("arbitrary",)),
        name="router_top2",
    )(h, gain.reshape(1, D_MODEL), whi, wlo, bias, tri)


ROW_PAYLOAD = D_MODEL + LANES


def _dispatch_kernel(dest_ref, h_ref, g_ref, gate_ref, xs_init, xs_hbm, rowbuf, sem):
    del xs_init
    base = pl.program_id(0) * ROW_TILE
    rowbuf[:, :D_MODEL] = _rmsnorm_rows(h_ref[...], g_ref[...])
    rowbuf[:, D_MODEL:] = gate_ref[...]

    def row_copy(r):
        return pltpu.make_async_copy(rowbuf.at[pl.ds(r, 1), :],
                                     xs_hbm.at[pl.ds(dest_ref[base + r], 1), :], sem)

    def start(r, carry):
        row_copy(r).start()
        return carry

    def wait(r, carry):
        row_copy(r).wait()
        return carry

    lax.fori_loop(0, ROW_TILE, start, 0, unroll=8)
    lax.fori_loop(0, ROW_TILE, wait, 0, unroll=8)


def _dispatch_rows(h, gain, gate_pad, dest, rows):
    t = h.shape[0]
    grid_spec = pltpu.PrefetchScalarGridSpec(
        num_scalar_prefetch=1,
        grid=(t // ROW_TILE,),
        in_specs=[pl.BlockSpec((ROW_TILE, D_MODEL), lambda i, d: (i, 0)),
                  pl.BlockSpec((1, D_MODEL), lambda i, d: (0, 0)),
                  pl.BlockSpec((ROW_TILE, LANES), lambda i, d: (i, 0)),
                  pl.BlockSpec(memory_space=pl.ANY)],
        out_specs=pl.BlockSpec(memory_space=pl.ANY),
        scratch_shapes=[pltpu.VMEM((ROW_TILE, ROW_PAYLOAD), F32), pltpu.SemaphoreType.DMA(())],
    )
    return pl.pallas_call(
        _dispatch_kernel,
        grid_spec=grid_spec,
        out_shape=jax.ShapeDtypeStruct((rows, ROW_PAYLOAD), F32),
        input_output_aliases={4: 0},
        compiler_params=_params(("arbitrary",)),
        name="moe_dispatch",
    )(dest, h, gain.reshape(1, D_MODEL), gate_pad, jnp.zeros((rows, ROW_PAYLOAD), F32))


def _combine_kernel(dest_ref, h_ref, g_ref, ys_hbm, o_ref, buf, sem, *, final_norm):
    base = pl.program_id(0) * ROW_TILE

    def row_copy(r):
        return pltpu.make_async_copy(ys_hbm.at[pl.ds(dest_ref[base + r], 1), :],
                                     buf.at[pl.ds(r, 1), :], sem)

    def start(r, carry):
        row_copy(r).start()
        return carry

    def wait(r, carry):
        row_copy(r).wait()
        return carry

    lax.fori_loop(0, ROW_TILE, start, 0, unroll=8)
    lax.fori_loop(0, ROW_TILE, wait, 0, unroll=8)
    out = h_ref[...] + buf[...]
    o_ref[...] = _rmsnorm_rows(out, g_ref[...]) if final_norm else out


def _combine_rows(h, ys, dest, final_gain):
    t = h.shape[0]
    final_norm = final_gain is not None
    gain = final_gain if final_norm else jnp.ones((D_MODEL,), F32)
    tile = pl.BlockSpec((ROW_TILE, D_MODEL), lambda i, d: (i, 0))
    grid_spec = pltpu.PrefetchScalarGridSpec(
        num_scalar_prefetch=1,
        grid=(t // ROW_TILE,),
        in_specs=[tile, pl.BlockSpec((1, D_MODEL), lambda i, d: (0, 0)),
                  pl.BlockSpec(memory_space=pl.ANY)],
        out_specs=tile,
        scratch_shapes=[pltpu.VMEM((ROW_TILE, D_MODEL), F32), pltpu.SemaphoreType.DMA(())],
    )
    return pl.pallas_call(
        functools.partial(_combine_kernel, final_norm=final_norm),
        grid_spec=grid_spec,
        out_shape=jax.ShapeDtypeStruct((t, D_MODEL), F32),
        compiler_params=_params(("arbitrary",)),
        name="moe_combine",
    )(dest, h, gain.reshape(1, D_MODEL), ys)


def _silu(a):
    return a / (1.0 + jnp.exp(-a))


def _moe_kernel(ea_ref, eb_ref, x_ref, w1a, w3a, w2a, w1b, w3b, w2b, o_ref):
    del ea_ref, eb_ref
    x = x_ref[:, :D_MODEL].astype(BF16)
    g = x_ref[:, D_MODEL:]
    ha = _silu(jnp.dot(x, w1a[...], preferred_element_type=F32))
    ha = ha * jnp.dot(x, w3a[...], preferred_element_type=F32) * g[:, 0:1]
    hb = _silu(jnp.dot(x, w1b[...], preferred_element_type=F32))
    hb = hb * jnp.dot(x, w3b[...], preferred_element_type=F32) * g[:, 1:2]
    out = jnp.dot(ha.astype(BF16), w2a[...], preferred_element_type=F32)
    out += jnp.dot(hb.astype(BF16), w2b[...], preferred_element_type=F32)
    o_ref[...] = out


def _expert_blocks(xs, block_ea, block_eb, w1, w3, w2):
    rows = xs.shape[0]
    n_blocks = rows // MOE_ROWS
    up = (None, D_MODEL, D_EXPERT)
    down = (None, D_EXPERT, D_MODEL)
    grid_spec = pltpu.PrefetchScalarGridSpec(
        num_scalar_prefetch=2,
        grid=(n_blocks,),
        in_specs=[pl.BlockSpec((MOE_ROWS, ROW_PAYLOAD), lambda i, ea, eb: (i, 0)),
                  pl.BlockSpec(up, lambda i, ea, eb: (ea[i], 0, 0)),
                  pl.BlockSpec(up, lambda i, ea, eb: (ea[i], 0, 0)),
                  pl.BlockSpec(down, lambda i, ea, eb: (ea[i], 0, 0)),
                  pl.BlockSpec(up, lambda i, ea, eb: (eb[i], 0, 0)),
                  pl.BlockSpec(up, lambda i, ea, eb: (eb[i], 0, 0)),
                  pl.BlockSpec(down, lambda i, ea, eb: (eb[i], 0, 0))],
        out_specs=pl.BlockSpec((MOE_ROWS, D_MODEL), lambda i, ea, eb: (i, 0)),
    )
    return pl.pallas_call(
        _moe_kernel,
        grid_spec=grid_spec,
        out_shape=jax.ShapeDtypeStruct((rows, D_MODEL), F32),
        compiler_params=_params(("arbitrary",)),
        name="expert_pair_blocks",
    )(block_ea, block_eb, xs, w1, w3, w2, w1, w3, w2)


def _hierarchical_moe(h, gain, wg, bg, we, be, w1, w3, w2, final_gain=None):
    t = h.shape[0]
    pad = LANES - N_EXPERT_GROUPS - N_EXPERTS
    wcat = jnp.concatenate(
        [wg, we.transpose(1, 0, 2).reshape(D_MODEL, N_EXPERTS), jnp.zeros((D_MODEL, pad), F32)], axis=1)
    bias = jnp.concatenate([bg.astype(F32), be.astype(F32).reshape(N_EXPERTS), jnp.zeros((pad,), F32)])
    cls, rank, gates, counts = _route_tokens(h, gain, wcat, bias.reshape(LANES, 1))

    counts = counts[:N_CLASSES, 0].astype(jnp.int32)
    padded = ((counts + MOE_ROWS - 1) // MOE_ROWS) * MOE_ROWS
    pends = jnp.cumsum(padded)
    pstarts = pends - padded
    classes = jnp.arange(N_CLASSES, dtype=jnp.int32)
    cls = cls.reshape(t)
    dest = rank.reshape(t) + jnp.sum(jnp.where(cls[:, None] == classes[None, :], pstarts[None, :], 0), axis=1)
    n_blocks = -(-(t + N_CLASSES * (MOE_ROWS - 1)) // MOE_ROWS)
    rows = n_blocks * MOE_ROWS
    block_start = jnp.arange(n_blocks, dtype=jnp.int32) * MOE_ROWS
    block_cls = jnp.minimum(jnp.sum((block_start[:, None] >= pends[None, :]).astype(jnp.int32), axis=1),
                            N_CLASSES - 1)
    pair_lo = jnp.asarray([p[0] for p in PAIRS], jnp.int32)
    pair_hi = jnp.asarray([p[1] for p in PAIRS], jnp.int32)
    grp = block_cls // len(PAIRS)
    block_ea = grp * EXPERTS_PER_GROUP + pair_lo[block_cls % len(PAIRS)]
    block_eb = grp * EXPERTS_PER_GROUP + pair_hi[block_cls % len(PAIRS)]

    gate_pad = jnp.pad(gates.T, ((0, 0), (0, LANES - gates.shape[0])))
    xs = _dispatch_rows(h, gain, gate_pad, dest, rows)
    ys = _expert_blocks(xs, block_ea, block_eb, w1, w3, w2)
    return _combine_rows(h, ys, dest, final_gain)


def _norm_proj_kernel(x_ref, g_ref, w_ref, o_ref):
    y = _rmsnorm_rows(x_ref[...], g_ref[...]).astype(BF16)
    o_ref[...] = jnp.dot(y, w_ref[...], preferred_element_type=F32)


def _norm_proj(x, gain, w):
    t, n = x.shape[0], w.shape[1]
    return pl.pallas_call(
        _norm_proj_kernel,
        grid=(t // ROW_TILE,),
        in_specs=[pl.BlockSpec((ROW_TILE, D_MODEL), lambda i: (i, 0)),
                  pl.BlockSpec((1, D_MODEL), lambda i: (0, 0)),
                  pl.BlockSpec((D_MODEL, n), lambda i: (0, 0))],
        out_specs=pl.BlockSpec((ROW_TILE, n), lambda i: (i, 0)),
        out_shape=jax.ShapeDtypeStruct((t, n), F32),
        compiler_params=_params(("parallel",)),
        name="norm_proj",
    )(x, gain.reshape(1, D_MODEL), w)


def _proj_residual_kernel(z_ref, w_ref, h_ref, o_ref):
    o_ref[...] = h_ref[...] + jnp.dot(z_ref[...], w_ref[...], preferred_element_type=F32)


def _proj_residual(z, w, h):
    t, n = h.shape
    k = z.shape[1]
    return pl.pallas_call(
        _proj_residual_kernel,
        grid=(t // ROW_TILE,),
        in_specs=[pl.BlockSpec((ROW_TILE, k), lambda i: (i, 0)),
                  pl.BlockSpec((k, n), lambda i: (0, 0)),
                  pl.BlockSpec((ROW_TILE, n), lambda i: (i, 0))],
        out_specs=pl.BlockSpec((ROW_TILE, n), lambda i: (i, 0)),
        out_shape=jax.ShapeDtypeStruct((t, n), F32),
        compiler_params=_params(("parallel",)),
        name="proj_residual",
    )(z, w, h)


def _pool_kernel(u_ref, wg_ref, sc_ref, z_ref):
    seq = u_ref.shape[0]
    row = lax.broadcasted_iota(jnp.int32, (seq, POOL_GROUP_DIM), 0)
    for g, window in enumerate(POOL_WINDOWS):
        cols = slice(g * POOL_GROUP_DIM, (g + 1) * POOL_GROUP_DIM)
        u = u_ref[:, cols]
        s = u
        step = 1
        while step < window:
            s = s + jnp.where(row >= step, pltpu.roll(s, step, 0), 0.0)
            step *= 2
        count = jnp.minimum(row + 1, window).astype(F32)
        pooled = (s / count - u).astype(BF16)
        z = jnp.dot(pooled, wg_ref[g], preferred_element_type=F32) * sc_ref[:, cols]
        z_ref[:, cols] = z.astype(BF16)


def _pool_mixer_inner(u, w_group, scale, batch, seq):
    blk = pl.BlockSpec((None, seq, D_MODEL), lambda b: (b, 0, 0))
    z = pl.pallas_call(
        _pool_kernel,
        grid=(batch,),
        in_specs=[blk,
                  pl.BlockSpec((N_POOL_GROUPS, POOL_GROUP_DIM, POOL_GROUP_DIM), lambda b: (0, 0, 0)),
                  pl.BlockSpec((1, D_MODEL), lambda b: (0, 0))],
        out_specs=blk,
        out_shape=jax.ShapeDtypeStruct((batch, seq, D_MODEL), BF16),
        compiler_params=_params(("parallel",)),
        name="causal_pool",
    )(u.reshape(batch, seq, D_MODEL), w_group, scale.reshape(1, D_MODEL))
    return z.reshape(batch * seq, D_MODEL)


def kernel(x, positions, norm_mix, norm_ffn, norm_final, attn_w_in, attn_w_out, pool_w_in, pool_w_group, pool_scale, pool_w_out, router_group_w, router_group_b, router_expert_w, router_expert_b, expert_w1, expert_w3, expert_w2):
    batch, seq, _ = x.shape
    t = batch * seq
    h = x.reshape(t, D_MODEL)
    w1 = expert_w1.astype(BF16)
    w3 = expert_w3.astype(BF16)
    w2 = expert_w2.astype(BF16)

    c, s1, s2 = _rope_tables(positions)
    qkv = _qkv_proj(h, norm_mix[0], attn_w_in[0].astype(BF16), c, s1, s2, batch, seq)
    outs, lses = [], []
    for g in range(N_ATTN_GROUPS):
        o, lse = _dilated_attention(qkv[3 * g], qkv[3 * g + 1], qkv[3 * g + 2], g, batch, seq)
        outs.append(o)
        lses.append(lse)
    h = _merge_out_proj(outs, lses, attn_w_out[0].astype(BF16), h, seq)
    h = _hierarchical_moe(h, norm_ffn[0], router_group_w[0], router_group_b[0],
                          router_expert_w[0], router_expert_b[0], w1[0], w3[0], w2[0])

    u = _norm_proj(h, norm_mix[1], pool_w_in[0].astype(BF16))
    z = _pool_mixer_inner(u, pool_w_group[0].astype(BF16), pool_scale[0], batch, seq)
    h = _proj_residual(z, pool_w_out[0].astype(BF16), h)
    out = _hierarchical_moe(h, norm_ffn[1], router_group_w[1], router_group_b[1],
                            router_expert_w[1], router_expert_b[1], w1[1], w3[1], w2[1],
                            final_gain=norm_final)
    return out.reshape(batch, seq, D_MODEL)
```

```python
import functools
import math

import jax
import jax.numpy as jnp
from jax import lax
from jax.experimental import pallas as pl
from jax.experimental.pallas import tpu as pltpu

F32 = jnp.float32
BF16 = jnp.bfloat16

D_MODEL = 1024
ATTN_PATTERNS = ((128, 1), (512, 4), (2048, 16))
N_ATTN_GROUPS = len(ATTN_PATTERNS)
HEADS = 8
HEAD_DIM = 64
ATTN_WIDTH = HEADS * HEAD_DIM
QKV_WIDTH = N_ATTN_GROUPS * 3 * ATTN_WIDTH
ROT_DIM = HEAD_DIM // 4
ROPE_THETA = 500000.0
ATTN_BLOCK = 128
POOL_WINDOWS = (2, 4, 8, 16)
N_POOL_GROUPS = len(POOL_WINDOWS)
POOL_GROUP_DIM = D_MODEL // N_POOL_GROUPS
N_EXPERT_GROUPS = 4
EXPERTS_PER_GROUP = 4
N_EXPERTS = N_EXPERT_GROUPS * EXPERTS_PER_GROUP
D_EXPERT = D_MODEL // 2
RMS_EPS = 1e-6

PAIRS = ((0, 1), (0, 2), (0, 3), (1, 2), (1, 3), (2, 3))
N_CLASSES = N_EXPERT_GROUPS * len(PAIRS)

ROW_TILE = 512
COL_TILE = 512
MOE_ROWS = 256
LANES = 128
VMEM_LIMIT = 48 * 1024 * 1024
MASKED = -1e30


def _params(sem):
    return pltpu.CompilerParams(dimension_semantics=sem, vmem_limit_bytes=VMEM_LIMIT)


def _rmsnorm_rows(x, g):
    ms = jnp.mean(x * x, axis=-1, keepdims=True)
    return (x * lax.rsqrt(ms + RMS_EPS)) * g


def _rope_kernel(pos_ref, freq_ref, c_ref, s1_ref, s2_ref):
    ang = pos_ref[...].astype(F32) * freq_ref[...]
    lane = lax.broadcasted_iota(jnp.int32, ang.shape, 1) % HEAD_DIM
    sin = jnp.sin(ang)
    c_ref[...] = jnp.cos(ang)
    s1_ref[...] = jnp.where(lane < ROT_DIM // 2, -sin, 0.0)
    s2_ref[...] = jnp.where(lane >= ROT_DIM // 2, sin, 0.0)


def _rope_tables(positions):
    t = positions.size
    half = ROT_DIM // 2
    inv = [ROPE_THETA ** (-(i * 2.0 / ROT_DIM)) for i in range(half)]
    per_head = inv + inv + [0.0] * (HEAD_DIM - ROT_DIM)
    freq = jnp.asarray([per_head * (LANES // HEAD_DIM)], F32)
    tm = 1024
    out = jax.ShapeDtypeStruct((t, LANES), F32)
    spec = pl.BlockSpec((tm, LANES), lambda i: (i, 0))
    return pl.pallas_call(
        _rope_kernel,
        grid=(t // tm,),
        in_specs=[pl.BlockSpec((tm, 1), lambda i: (i, 0)),
                  pl.BlockSpec((1, LANES), lambda i: (0, 0))],
        out_specs=[spec, spec, spec],
        out_shape=[out, out, out],
        compiler_params=_params(("parallel",)),
        name="rope_tables",
    )(positions.reshape(t, 1), freq)


def _qkv_kernel(x_ref, g_ref, w_ref, c_ref, s1_ref, s2_ref, *rest):
    o_refs, (y_sc, acc_sc) = rest[:-2], rest[-2:]
    y_sc[...] = _rmsnorm_rows(x_ref[...], g_ref[...]).astype(BF16)
    chunks = COL_TILE // LANES
    for col in range(QKV_WIDTH // COL_TILE):
        dil = ATTN_PATTERNS[col // 3][1]
        n = ROW_TILE // dil
        buf = col % 2
        acc = jnp.dot(y_sc[...], w_ref[:, col * COL_TILE:(col + 1) * COL_TILE],
                      preferred_element_type=F32)
        for c in range(chunks):
            acc_sc[buf, c] = acc[:, c * LANES:(c + 1) * LANES]
        rotary = col % 3 != 2
        for p in range(dil):
            rows = pl.ds(p, n, stride=dil) if dil > 1 else slice(None)
            if rotary:
                cos, sin_up, sin_dn = c_ref[rows, :], s1_ref[rows, :], s2_ref[rows, :]
            for c in range(chunks):
                a = acc_sc[buf, c, rows, :]
                if rotary:
                    a = (a * cos + pltpu.roll(a, LANES - ROT_DIM // 2, 1) * sin_up
                         + pltpu.roll(a, ROT_DIM // 2, 1) * sin_dn)
                if col % 3 == 0:
                    a = a * (1.0 / math.sqrt(HEAD_DIM))
                lanes = slice(p * COL_TILE + c * LANES, p * COL_TILE + (c + 1) * LANES)
                o_refs[col][:, lanes] = a.astype(BF16)


def _qkv_proj(x, gain, w, c, s1, s2, batch, seq):
    t = x.shape[0]
    tiles = seq // ROW_TILE
    tab = pl.BlockSpec((ROW_TILE, LANES), lambda i: (i, 0))
    out_specs, out_shapes = [], []
    for _, dil in ATTN_PATTERNS:
        for _ in range(3):
            out_specs.append(pl.BlockSpec((None, ROW_TILE // dil, dil * ATTN_WIDTH),
                                          lambda i: (i // tiles, i % tiles, 0)))
            out_shapes.append(jax.ShapeDtypeStruct((batch, seq // dil, dil * ATTN_WIDTH), BF16))
    return pl.pallas_call(
        _qkv_kernel,
        grid=(t // ROW_TILE,),
        in_specs=[pl.BlockSpec((ROW_TILE, D_MODEL), lambda i: (i, 0)),
                  pl.BlockSpec((1, D_MODEL), lambda i: (0, 0)),
                  pl.BlockSpec((D_MODEL, QKV_WIDTH), lambda i: (0, 0)),
                  tab, tab, tab],
        out_specs=out_specs,
        out_shape=out_shapes,
        scratch_shapes=[pltpu.VMEM((ROW_TILE, D_MODEL), BF16),
                        pltpu.VMEM((2, COL_TILE // LANES, ROW_TILE, LANES), F32)],
        compiler_params=_params(("parallel",)),
        name="qkv_proj",
    )(x, gain.reshape(1, D_MODEL), w, c, s1, s2)


def _attn_kernel(q_ref, kp_ref, kc_ref, vp_ref, vc_ref, o_ref, l_ref):
    j = pl.program_id(2)
    blk = ATTN_BLOCK
    row = lax.broadcasted_iota(jnp.int32, (blk, 2 * blk), 0)
    col = lax.broadcasted_iota(jnp.int32, (blk, 2 * blk), 1)
    lo = jnp.maximum(row, jnp.where(j == 0, blk, 0))
    bias = jnp.where(col >= lo, jnp.where(col <= row + blk, 0.0, MASKED), MASKED)
    first = lax.broadcasted_iota(jnp.int32, (blk, LANES), 1) < HEAD_DIM
    keep = (jnp.where(first, 1.0, 0.0).astype(BF16), jnp.where(first, 0.0, 1.0).astype(BF16))
    for pair in range(HEADS * HEAD_DIM // LANES):
        sl = slice(pair * LANES, (pair + 1) * LANES)
        q2 = q_ref[:, sl]
        k2 = jnp.concatenate([kp_ref[:, sl], kc_ref[:, sl]], axis=0)
        v2 = jnp.concatenate([vp_ref[:, sl], vc_ref[:, sl]], axis=0)
        o_half, r_half, l_half = [], [], []
        for half in range(2):
            s = lax.dot_general(q2 * keep[half], k2, (((1,), (1,)), ((), ())),
                                preferred_element_type=F32) + bias
            m = jnp.max(s, axis=-1, keepdims=True)
            p = jnp.exp(s - m)
            den = jnp.sum(p, axis=-1, keepdims=True)
            o_half.append(jnp.dot(p.astype(BF16), v2, preferred_element_type=F32))
            r_half.append(1.0 / den)
            l_half.append(m + jnp.log(den))
        o = jnp.where(first, o_half[0], o_half[1]) * jnp.where(first, r_half[0], r_half[1])
        o_ref[:, sl] = o.astype(BF16)
        l_ref[:, sl] = jnp.where(first, l_half[0], l_half[1])


def _dilated_attention(q, k, v, group, batch, seq):
    _, dil = ATTN_PATTERNS[group]
    length = seq // dil
    nb = length // ATTN_BLOCK
    cur = pl.BlockSpec((None, ATTN_BLOCK, ATTN_WIDTH), lambda b, p, j: (b, j, p))
    prev = pl.BlockSpec((None, ATTN_BLOCK, ATTN_WIDTH), lambda b, p, j: (b, jnp.maximum(j - 1, 0), p))
    return pl.pallas_call(
        _attn_kernel,
        grid=(batch, dil, nb),
        in_specs=[cur, prev, cur, prev, cur],
        out_specs=[cur, cur],
        out_shape=[jax.ShapeDtypeStruct((batch, length, dil * ATTN_WIDTH), BF16),
                   jax.ShapeDtypeStruct((batch, length, dil * ATTN_WIDTH), F32)],
        compiler_params=_params(("parallel", "parallel", "arbitrary")),
        name=f"dilated_attn_g{group}",
    )(q, k, k, v, v)


def _natural_rows(ref, stage, dil):
    if dil == 1:
        return ref[...].astype(F32)
    n = ROW_TILE // dil
    chunks = ATTN_WIDTH // LANES
    for p in range(dil):
        for c in range(chunks):
            lanes = slice(p * ATTN_WIDTH + c * LANES, p * ATTN_WIDTH + (c + 1) * LANES)
            stage[c, pl.ds(p, n, stride=dil), :] = ref[:, lanes].astype(F32)
    return jnp.concatenate([stage[c] for c in range(chunks)], axis=1)


def _merge_kernel(o0, o1, o2, l0, l1, l2, w_ref, h_ref, out_ref, so1, so2, sl1, sl2):
    dils = [d for _, d in ATTN_PATTERNS]
    a = _natural_rows(l0, None, dils[0])
    b = _natural_rows(l1, sl1, dils[1])
    c = _natural_rows(l2, sl2, dils[2])
    m = jnp.maximum(jnp.maximum(a, b), c)
    ea, eb, ec = jnp.exp(a - m), jnp.exp(b - m), jnp.exp(c - m)
    mix = ea * _natural_rows(o0, None, dils[0])
    mix += eb * _natural_rows(o1, so1, dils[1])
    mix += ec * _natural_rows(o2, so2, dils[2])
    mix = mix / (ea + eb + ec)
    out_ref[...] = h_ref[...] + jnp.dot(mix.astype(BF16), w_ref[...], preferred_element_type=F32)


def _merge_out_proj(outs, lses, w, h, seq):
    t = h.shape[0]
    tiles = seq // ROW_TILE
    slabs = [pl.BlockSpec((None, ROW_TILE // d, d * ATTN_WIDTH), lambda i: (i // tiles, i % tiles, 0))
             for _, d in ATTN_PATTERNS]
    full = pl.BlockSpec((ROW_TILE, D_MODEL), lambda i: (i, 0))
    stage = pltpu.VMEM((ATTN_WIDTH // LANES, ROW_TILE, LANES), F32)
    return pl.pallas_call(
        _merge_kernel,
        grid=(t // ROW_TILE,),
        in_specs=slabs + slabs + [pl.BlockSpec((ATTN_WIDTH, D_MODEL), lambda i: (0, 0)), full],
        out_specs=full,
        out_shape=jax.ShapeDtypeStruct((t, D_MODEL), F32),
        scratch_shapes=[stage, stage, stage, stage],
        compiler_params=_params(("parallel",)),
        name="attn_merge_out",
    )(*outs, *lses, w, h)


CLASS_ROWS = 32


def _router_kernel(x_ref, g_ref, whi_ref, wlo_ref, b_ref, tri_ref,
                   cls_ref, rank_ref, gate_ref, count_ref, run_sc):
    @pl.when(pl.program_id(0) == 0)
    def _():
        run_sc[...] = jnp.zeros_like(run_sc)

    y = _rmsnorm_rows(x_ref[...], g_ref[...])
    y_hi = y.astype(BF16)
    y_lo = (y - y_hi.astype(F32)).astype(BF16)
    nt = (((1,), (1,)), ((), ()))
    whi = whi_ref[...]
    lg = lax.dot_general(whi, y_hi, nt, preferred_element_type=F32)
    lg += lax.dot_general(whi, y_lo, nt, preferred_element_type=F32)
    lg += lax.dot_general(wlo_ref[...], y_hi, nt, preferred_element_type=F32)
    lg = lg + b_ref[...]
    row = [lg[r:r + 1, :] for r in range(N_EXPERT_GROUPS + N_EXPERTS)]

    best, grp = row[0], jnp.zeros(row[0].shape, jnp.int32)
    for g in range(1, N_EXPERT_GROUPS):
        better = row[g] > best
        grp = jnp.where(better, g, grp)
        best = jnp.where(better, row[g], best)
    gate1 = 1.0 / sum(jnp.exp(row[g] - best) for g in range(N_EXPERT_GROUPS))

    le = []
    for e in range(EXPERTS_PER_GROUP):
        v = row[N_EXPERT_GROUPS + e]
        for g in range(1, N_EXPERT_GROUPS):
            v = jnp.where(grp == g, row[N_EXPERT_GROUPS + g * EXPERTS_PER_GROUP + e], v)
        le.append(v)
    v1, i1 = le[0], jnp.zeros(grp.shape, jnp.int32)
    for e in range(1, EXPERTS_PER_GROUP):
        better = le[e] > v1
        i1 = jnp.where(better, e, i1)
        v1 = jnp.where(better, le[e], v1)
    v2, i2 = jnp.full(v1.shape, -jnp.inf, F32), jnp.zeros(grp.shape, jnp.int32)
    for e in range(EXPERTS_PER_GROUP):
        cand = jnp.where(i1 == e, -jnp.inf, le[e])
        better = cand > v2
        i2 = jnp.where(better, e, i2)
        v2 = jnp.where(better, cand, v2)
    ratio = jnp.exp(v2 - v1)
    g_first = gate1 / (1.0 + ratio)
    g_second = gate1 * ratio / (1.0 + ratio)
    first_low = i1 < i2
    lo = jnp.where(first_low, i1, i2)
    hi = jnp.where(first_low, i2, i1)
    pair = jnp.right_shift(lo * (2 * EXPERTS_PER_GROUP - 1 - lo), 1) + (hi - lo - 1)
    cls = grp * len(PAIRS) + pair
    cls_ref[...] = cls
    gate_ref[0:1, :] = jnp.where(first_low, g_first, g_second)
    gate_ref[1:2, :] = jnp.where(first_low, g_second, g_first)

    onehot = lax.broadcasted_iota(jnp.int32, (CLASS_ROWS, ROW_TILE), 0) == cls
    oh = jnp.where(onehot, 1.0, 0.0)
    earlier = jnp.dot(oh.astype(BF16), tri_ref[...], preferred_element_type=F32)
    running = run_sc[:, 0:1]
    rank_ref[...] = jnp.sum(oh * (earlier + running), axis=0, keepdims=True).astype(jnp.int32)
    run_sc[...] = run_sc[...] + jnp.sum(oh, axis=1, keepdims=True)
    count_ref[...] = run_sc[...]


def _route_tokens(h, gain, wcat, bias):
    t = h.shape[0]
    wt = wcat.T
    whi = wt.astype(BF16)
    wlo = (wt - whi.astype(F32)).astype(BF16)
    idx = jnp.arange(ROW_TILE, dtype=jnp.int32)
    tri = (idx[:, None] < idx[None, :]).astype(BF16)
    wspec = pl.BlockSpec((LANES, D_MODEL), lambda i: (0, 0))
    row_i = pl.BlockSpec((1, ROW_TILE), lambda i: (0, i))
    return pl.pallas_call(
        _router_kernel,
        grid=(t // ROW_TILE,),
        in_specs=[pl.BlockSpec((ROW_TILE, D_MODEL), lambda i: (i, 0)),
                  pl.BlockSpec((1, D_MODEL), lambda i: (0, 0)), wspec, wspec,
                  pl.BlockSpec((LANES, 1), lambda i: (0, 0)),
                  pl.BlockSpec((ROW_TILE, ROW_TILE), lambda i: (0, 0))],
        out_specs=[row_i, row_i, pl.BlockSpec((2, ROW_TILE), lambda i: (0, i)),
                   pl.BlockSpec((CLASS_ROWS, LANES), lambda i: (0, 0))],
        out_shape=[jax.ShapeDtypeStruct((1, t), jnp.int32),
                   jax.ShapeDtypeStruct((1, t), jnp.int32),
                   jax.ShapeDtypeStruct((2, t), F32),
                   jax.ShapeDtypeStruct((CLASS_ROWS, LANES), F32)],
        scratch_shapes=[pltpu.VMEM((CLASS_ROWS, LANES), F32)],
        compiler_params=_params(("arbitrary",)),
        name="router_top2",
    )(h, gain.reshape(1, D_MODEL), whi, wlo, bias, tri)


ROW_PAYLOAD = D_MODEL + LANES


def _dispatch_kernel(dest_ref, h_ref, g_ref, gate_ref, xs_init, xs_hbm, rowbuf, sem):
    del xs_init
    base = pl.program_id(0) * ROW_TILE
    rowbuf[:, :D_MODEL] = _rmsnorm_rows(h_ref[...], g_ref[...])
    rowbuf[:, D_MODEL:] = gate_ref[...]

    def row_copy(r):
        return pltpu.make_async_copy(rowbuf.at[pl.ds(r, 1), :],
                                     xs_hbm.at[pl.ds(dest_ref[base + r], 1), :], sem)

    def start(r, carry):
        row_copy(r).start()
        return carry

    def wait(r, carry):
        row_copy(r).wait()
        return carry

    lax.fori_loop(0, ROW_TILE, start, 0, unroll=8)
    lax.fori_loop(0, ROW_TILE, wait, 0, unroll=8)


def _dispatch_rows(h, gain, gate_pad, dest, rows):
    t = h.shape[0]
    grid_spec = pltpu.PrefetchScalarGridSpec(
        num_scalar_prefetch=1,
        grid=(t // ROW_TILE,),
        in_specs=[pl.BlockSpec((ROW_TILE, D_MODEL), lambda i, d: (i, 0)),
                  pl.BlockSpec((1, D_MODEL), lambda i, d: (0, 0)),
                  pl.BlockSpec((ROW_TILE, LANES), lambda i, d: (i, 0)),
                  pl.BlockSpec(memory_space=pl.ANY)],
        out_specs=pl.BlockSpec(memory_space=pl.ANY),
        scratch_shapes=[pltpu.VMEM((ROW_TILE, ROW_PAYLOAD), F32), pltpu.SemaphoreType.DMA(())],
    )
    return pl.pallas_call(
        _dispatch_kernel,
        grid_spec=grid_spec,
        out_shape=jax.ShapeDtypeStruct((rows, ROW_PAYLOAD), F32),
        input_output_aliases={4: 0},
        compiler_params=_params(("arbitrary",)),
        name="moe_dispatch",
    )(dest, h, gain.reshape(1, D_MODEL), gate_pad, jnp.zeros((rows, ROW_PAYLOAD), F32))


def _combine_kernel(dest_ref, h_ref, g_ref, ys_hbm, o_ref, buf, sem, *, final_norm):
    base = pl.program_id(0) * ROW_TILE

    def row_copy(r):
        return pltpu.make_async_copy(ys_hbm.at[pl.ds(dest_ref[base + r], 1), :],
                                     buf.at[pl.ds(r, 1), :], sem)

    def start(r, carry):
        row_copy(r).start()
        return carry

    def wait(r, carry):
        row_copy(r).wait()
        return carry

    lax.fori_loop(0, ROW_TILE, start, 0, unroll=8)
    lax.fori_loop(0, ROW_TILE, wait, 0, unroll=8)
    out = h_ref[...] + buf[...]
    o_ref[...] = _rmsnorm_rows(out, g_ref[...]) if final_norm else out


def _combine_rows(h, ys, dest, final_gain):
    t = h.shape[0]
    final_norm = final_gain is not None
    gain = final_gain if final_norm else jnp.ones((D_MODEL,), F32)
    tile = pl.BlockSpec((ROW_TILE, D_MODEL), lambda i, d: (i, 0))
    grid_spec = pltpu.PrefetchScalarGridSpec(
        num_scalar_prefetch=1,
        grid=(t // ROW_TILE,),
        in_specs=[tile, pl.BlockSpec((1, D_MODEL), lambda i, d: (0, 0)),
                  pl.BlockSpec(memory_space=pl.ANY)],
        out_specs=tile,
        scratch_shapes=[pltpu.VMEM((ROW_TILE, D_MODEL), F32), pltpu.SemaphoreType.DMA(())],
    )
    return pl.pallas_call(
        functools.partial(_combine_kernel, final_norm=final_norm),
        grid_spec=grid_spec,
        out_shape=jax.ShapeDtypeStruct((t, D_MODEL), F32),
        compiler_params=_params(("arbitrary",)),
        name="moe_combine",
    )(dest, h, gain.reshape(1, D_MODEL), ys)


def _silu(a):
    return a / (1.0 + jnp.exp(-a))


def _moe_kernel(ea_ref, eb_ref, x_ref, w1a, w3a, w2a, w1b, w3b, w2b, o_ref):
    del ea_ref, eb_ref
    x = x_ref[:, :D_MODEL].astype(BF16)
    g = x_ref[:, D_MODEL:]
    ha = _silu(jnp.dot(x, w1a[...], preferred_element_type=F32))
    ha = ha * jnp.dot(x, w3a[...], preferred_element_type=F32) * g[:, 0:1]
    hb = _silu(jnp.dot(x, w1b[...], preferred_element_type=F32))
    hb = hb * jnp.dot(x, w3b[...], preferred_element_type=F32) * g[:, 1:2]
    out = jnp.dot(ha.astype(BF16), w2a[...], preferred_element_type=F32)
    out += jnp.dot(hb.astype(BF16), w2b[...], preferred_element_type=F32)
    o_ref[...] = out


def _expert_blocks(xs, block_ea, block_eb, w1, w3, w2, layer):
    rows = xs.shape[0]
    n_blocks = rows // MOE_ROWS
    up = (None, None, D_MODEL, D_EXPERT)
    down = (None, None, D_EXPERT, D_MODEL)
    grid_spec = pltpu.PrefetchScalarGridSpec(
        num_scalar_prefetch=2,
        grid=(n_blocks,),
        in_specs=[pl.BlockSpec((MOE_ROWS, ROW_PAYLOAD), lambda i, ea, eb: (i, 0)),
                  pl.BlockSpec(up, lambda i, ea, eb: (layer, ea[i], 0, 0)),
                  pl.BlockSpec(up, lambda i, ea, eb: (layer, ea[i], 0, 0)),
                  pl.BlockSpec(down, lambda i, ea, eb: (layer, ea[i], 0, 0)),
                  pl.BlockSpec(up, lambda i, ea, eb: (layer, eb[i], 0, 0)),
                  pl.BlockSpec(up, lambda i, ea, eb: (layer, eb[i], 0, 0)),
                  pl.BlockSpec(down, lambda i, ea, eb: (layer, eb[i], 0, 0))],
        out_specs=pl.BlockSpec((MOE_ROWS, D_MODEL), lambda i, ea, eb: (i, 0)),
    )
    return pl.pallas_call(
        _moe_kernel,
        grid_spec=grid_spec,
        out_shape=jax.ShapeDtypeStruct((rows, D_MODEL), F32),
        compiler_params=_params(("arbitrary",)),
        name="expert_pair_blocks",
    )(block_ea, block_eb, xs, w1, w3, w2, w1, w3, w2)


def _hierarchical_moe(h, gain, wg, bg, we, be, w1, w3, w2, layer, final_gain=None):
    t = h.shape[0]
    pad = LANES - N_EXPERT_GROUPS - N_EXPERTS
    wcat = jnp.concatenate(
        [wg, we.transpose(1, 0, 2).reshape(D_MODEL, N_EXPERTS), jnp.zeros((D_MODEL, pad), F32)], axis=1)
    bias = jnp.concatenate([bg.astype(F32), be.astype(F32).reshape(N_EXPERTS), jnp.zeros((pad,), F32)])
    cls, rank, gates, counts = _route_tokens(h, gain, wcat, bias.reshape(LANES, 1))

    counts = counts[:N_CLASSES, 0].astype(jnp.int32)
    padded = ((counts + MOE_ROWS - 1) // MOE_ROWS) * MOE_ROWS
    pends = jnp.cumsum(padded)
    pstarts = pends - padded
    classes = jnp.arange(N_CLASSES, dtype=jnp.int32)
    cls = cls.reshape(t)
    dest = rank.reshape(t) + jnp.sum(jnp.where(cls[:, None] == classes[None, :], pstarts[None, :], 0), axis=1)
    n_blocks = -(-(t + N_CLASSES * (MOE_ROWS - 1)) // MOE_ROWS)
    rows = n_blocks * MOE_ROWS
    block_start = jnp.arange(n_blocks, dtype=jnp.int32) * MOE_ROWS
    block_cls = jnp.minimum(jnp.sum((block_start[:, None] >= pends[None, :]).astype(jnp.int32), axis=1),
                            N_CLASSES - 1)
    pair_lo = jnp.asarray([p[0] for p in PAIRS], jnp.int32)
    pair_hi = jnp.asarray([p[1] for p in PAIRS], jnp.int32)
    grp = block_cls // len(PAIRS)
    block_ea = grp * EXPERTS_PER_GROUP + pair_lo[block_cls % len(PAIRS)]
    block_eb = grp * EXPERTS_PER_GROUP + pair_hi[block_cls % len(PAIRS)]

    gate_pad = jnp.pad(gates.T, ((0, 0), (0, LANES - gates.shape[0])))
    xs = _dispatch_rows(h, gain, gate_pad, dest, rows)
    ys = _expert_blocks(xs, block_ea, block_eb, w1, w3, w2, layer)
    return _combine_rows(h, ys, dest, final_gain)


def _norm_proj_kernel(x_ref, g_ref, w_ref, o_ref):
    y = _rmsnorm_rows(x_ref[...], g_ref[...]).astype(BF16)
    o_ref[...] = jnp.dot(y, w_ref[...], preferred_element_type=F32)


def _norm_proj(x, gain, w):
    t, n = x.shape[0], w.shape[1]
    return pl.pallas_call(
        _norm_proj_kernel,
        grid=(t // ROW_TILE,),
        in_specs=[pl.BlockSpec((ROW_TILE, D_MODEL), lambda i: (i, 0)),
                  pl.BlockSpec((1, D_MODEL), lambda i: (0, 0)),
                  pl.BlockSpec((D_MODEL, n), lambda i: (0, 0))],
        out_specs=pl.BlockSpec((ROW_TILE, n), lambda i: (i, 0)),
        out_shape=jax.ShapeDtypeStruct((t, n), F32),
        compiler_params=_params(("parallel",)),
        name="norm_proj",
    )(x, gain.reshape(1, D_MODEL), w)


def _proj_residual_kernel(z_ref, w_ref, h_ref, o_ref):
    o_ref[...] = h_ref[...] + jnp.dot(z_ref[...], w_ref[...], preferred_element_type=F32)


def _proj_residual(z, w, h):
    t, n = h.shape
    k = z.shape[1]
    return pl.pallas_call(
        _proj_residual_kernel,
        grid=(t // ROW_TILE,),
        in_specs=[pl.BlockSpec((ROW_TILE, k), lambda i: (i, 0)),
                  pl.BlockSpec((k, n), lambda i: (0, 0)),
                  pl.BlockSpec((ROW_TILE, n), lambda i: (i, 0))],
        out_specs=pl.BlockSpec((ROW_TILE, n), lambda i: (i, 0)),
        out_shape=jax.ShapeDtypeStruct((t, n), F32),
        compiler_params=_params(("parallel",)),
        name="proj_residual",
    )(z, w, h)


def _pool_kernel(u_ref, wg_ref, sc_ref, z_ref):
    seq = u_ref.shape[0]
    row = lax.broadcasted_iota(jnp.int32, (seq, POOL_GROUP_DIM), 0)
    for g, window in enumerate(POOL_WINDOWS):
        cols = slice(g * POOL_GROUP_DIM, (g + 1) * POOL_GROUP_DIM)
        u = u_ref[:, cols]
        s = u
        step = 1
        while step < window:
            s = s + jnp.where(row >= step, pltpu.roll(s, step, 0), 0.0)
            step *= 2
        count = jnp.minimum(row + 1, window).astype(F32)
        pooled = (s / count - u).astype(BF16)
        z = jnp.dot(pooled, wg_ref[g], preferred_element_type=F32) * sc_ref[:, cols]
        z_ref[:, cols] = z.astype(BF16)


def _pool_mixer_inner(u, w_group, scale, batch, seq):
    blk = pl.BlockSpec((None, seq, D_MODEL), lambda b: (b, 0, 0))
    z = pl.pallas_call(
        _pool_kernel,
        grid=(batch,),
        in_specs=[blk,
                  pl.BlockSpec((N_POOL_GROUPS, POOL_GROUP_DIM, POOL_GROUP_DIM), lambda b: (0, 0, 0)),
                  pl.BlockSpec((1, D_MODEL), lambda b: (0, 0))],
        out_specs=blk,
        out_shape=jax.ShapeDtypeStruct((batch, seq, D_MODEL), BF16),
        compiler_params=_params(("parallel",)),
        name="causal_pool",
    )(u.reshape(batch, seq, D_MODEL), w_group, scale.reshape(1, D_MODEL))
    return z.reshape(batch * seq, D_MODEL)


def kernel(x, positions, norm_mix, norm_ffn, norm_final, attn_w_in, attn_w_out, pool_w_in, pool_w_group, pool_scale, pool_w_out, router_group_w, router_group_b, router_expert_w, router_expert_b, expert_w1, expert_w3, expert_w2):
    batch, seq, _ = x.shape
    t = batch * seq
    h = x.reshape(t, D_MODEL)
    w1 = expert_w1.astype(BF16)
    w3 = expert_w3.astype(BF16)
    w2 = expert_w2.astype(BF16)

    c, s1, s2 = _rope_tables(positions)
    qkv = _qkv_proj(h, norm_mix[0], attn_w_in[0].astype(BF16), c, s1, s2, batch, seq)
    outs, lses = [], []
    for g in range(N_ATTN_GROUPS):
        o, lse = _dilated_attention(qkv[3 * g], qkv[3 * g + 1], qkv[3 * g + 2], g, batch, seq)
        outs.append(o)
        lses.append(lse)
    h = _merge_out_proj(outs, lses, attn_w_out[0].astype(BF16), h, seq)
    h = _hierarchical_moe(h, norm_ffn[0], router_group_w[0], router_group_b[0],
                          router_expert_w[0], router_expert_b[0], w1, w3, w2, 0)

    u = _norm_proj(h, norm_mix[1], pool_w_in[0].astype(BF16))
    z = _pool_mixer_inner(u, pool_w_group[0].astype(BF16), pool_scale[0], batch, seq)
    h = _proj_residual(z, pool_w_out[0].astype(BF16), h)
    out = _hierarchical_moe(h, norm_ffn[1], router_group_w[1], router_group_b[1],
                            router_expert_w[1], router_expert_b[1], w1, w3, w2, 1,
                            final_gain=norm_final)
    return out.reshape(batch, seq, D_MODEL)
```

```python
import functools
import math

import jax
import jax.numpy as jnp
from jax import lax
from jax.experimental import pallas as pl
from jax.experimental.pallas import tpu as pltpu

F32 = jnp.float32
BF16 = jnp.bfloat16

D_MODEL = 1024
ATTN_PATTERNS = ((128, 1), (512, 4), (2048, 16))
N_ATTN_GROUPS = len(ATTN_PATTERNS)
HEADS = 8
HEAD_DIM = 64
ATTN_WIDTH = HEADS * HEAD_DIM
QKV_WIDTH = N_ATTN_GROUPS * 3 * ATTN_WIDTH
ROT_DIM = HEAD_DIM // 4
ROPE_THETA = 500000.0
ATTN_BLOCK = 128
POOL_WINDOWS = (2, 4, 8, 16)
N_POOL_GROUPS = len(POOL_WINDOWS)
POOL_GROUP_DIM = D_MODEL // N_POOL_GROUPS
N_EXPERT_GROUPS = 4
EXPERTS_PER_GROUP = 4
N_EXPERTS = N_EXPERT_GROUPS * EXPERTS_PER_GROUP
D_EXPERT = D_MODEL // 2
RMS_EPS = 1e-6

PAIRS = ((0, 1), (0, 2), (0, 3), (1, 2), (1, 3), (2, 3))
N_CLASSES = N_EXPERT_GROUPS * len(PAIRS)

ROW_TILE = 512
COL_TILE = 512
MOE_ROWS = 256
LANES = 128
VMEM_LIMIT = 48 * 1024 * 1024
MASKED = -1e30


def _params(sem):
    return pltpu.CompilerParams(dimension_semantics=sem, vmem_limit_bytes=VMEM_LIMIT)


def _rmsnorm_rows(x, g):
    ms = jnp.mean(x * x, axis=-1, keepdims=True)
    return (x * lax.rsqrt(ms + RMS_EPS)) * g


def _rope_kernel(pos_ref, freq_ref, c_ref, s1_ref, s2_ref):
    ang = pos_ref[...].astype(F32) * freq_ref[...]
    lane = lax.broadcasted_iota(jnp.int32, ang.shape, 1) % HEAD_DIM
    sin = jnp.sin(ang)
    c_ref[...] = jnp.cos(ang)
    s1_ref[...] = jnp.where(lane < ROT_DIM // 2, -sin, 0.0)
    s2_ref[...] = jnp.where(lane >= ROT_DIM // 2, sin, 0.0)


def _rope_tables(positions):
    t = positions.size
    half = ROT_DIM // 2
    inv = [ROPE_THETA ** (-(i * 2.0 / ROT_DIM)) for i in range(half)]
    per_head = inv + inv + [0.0] * (HEAD_DIM - ROT_DIM)
    freq = jnp.asarray([per_head * (LANES // HEAD_DIM)], F32)
    tm = 1024
    out = jax.ShapeDtypeStruct((t, LANES), F32)
    spec = pl.BlockSpec((tm, LANES), lambda i: (i, 0))
    return pl.pallas_call(
        _rope_kernel,
        grid=(t // tm,),
        in_specs=[pl.BlockSpec((tm, 1), lambda i: (i, 0)),
                  pl.BlockSpec((1, LANES), lambda i: (0, 0))],
        out_specs=[spec, spec, spec],
        out_shape=[out, out, out],
        compiler_params=_params(("parallel",)),
        name="rope_tables",
    )(positions.reshape(t, 1), freq)


def _qkv_kernel(x_ref, g_ref, w_ref, c_ref, s1_ref, s2_ref, *rest):
    o_refs, (y_sc, acc_sc) = rest[:-2], rest[-2:]
    y_sc[...] = _rmsnorm_rows(x_ref[...], g_ref[...]).astype(BF16)
    chunks = COL_TILE // LANES
    for col in range(QKV_WIDTH // COL_TILE):
        dil = ATTN_PATTERNS[col // 3][1]
        n = ROW_TILE // dil
        buf = col % 2
        acc = jnp.dot(y_sc[...], w_ref[:, col * COL_TILE:(col + 1) * COL_TILE],
                      preferred_element_type=F32)
        for c in range(chunks):
            acc_sc[buf, c] = acc[:, c * LANES:(c + 1) * LANES]
        rotary = col % 3 != 2
        for p in range(dil):
            rows = pl.ds(p, n, stride=dil) if dil > 1 else slice(None)
            if rotary:
                cos, sin_up, sin_dn = c_ref[rows, :], s1_ref[rows, :], s2_ref[rows, :]
            for c in range(chunks):
                a = acc_sc[buf, c, rows, :]
                if rotary:
                    a = (a * cos + pltpu.roll(a, LANES - ROT_DIM // 2, 1) * sin_up
                         + pltpu.roll(a, ROT_DIM // 2, 1) * sin_dn)
                if col % 3 == 0:
                    a = a * (1.0 / math.sqrt(HEAD_DIM))
                lanes = slice(p * COL_TILE + c * LANES, p * COL_TILE + (c + 1) * LANES)
                o_refs[col][:, lanes] = a.astype(BF16)


def _qkv_proj(x, gain, w, c, s1, s2, batch, seq):
    t = x.shape[0]
    tiles = seq // ROW_TILE
    tab = pl.BlockSpec((ROW_TILE, LANES), lambda i: (i, 0))
    out_specs, out_shapes = [], []
    for _, dil in ATTN_PATTERNS:
        for _ in range(3):
            out_specs.append(pl.BlockSpec((None, ROW_TILE // dil, dil * ATTN_WIDTH),
                                          lambda i: (i // tiles, i % tiles, 0)))
            out_shapes.append(jax.ShapeDtypeStruct((batch, seq // dil, dil * ATTN_WIDTH), BF16))
    return pl.pallas_call(
        _qkv_kernel,
        grid=(t // ROW_TILE,),
        in_specs=[pl.BlockSpec((ROW_TILE, D_MODEL), lambda i: (i, 0)),
                  pl.BlockSpec((1, D_MODEL), lambda i: (0, 0)),
                  pl.BlockSpec((D_MODEL, QKV_WIDTH), lambda i: (0, 0)),
                  tab, tab, tab],
        out_specs=out_specs,
        out_shape=out_shapes,
        scratch_shapes=[pltpu.VMEM((ROW_TILE, D_MODEL), BF16),
                        pltpu.VMEM((2, COL_TILE // LANES, ROW_TILE, LANES), F32)],
        compiler_params=_params(("parallel",)),
        name="qkv_proj",
    )(x, gain.reshape(1, D_MODEL), w, c, s1, s2)


def _attn_kernel(q_ref, kp_ref, kc_ref, vp_ref, vc_ref, o_ref, l_ref):
    j = pl.program_id(2)
    blk = ATTN_BLOCK
    row = lax.broadcasted_iota(jnp.int32, (blk, 2 * blk), 0)
    col = lax.broadcasted_iota(jnp.int32, (blk, 2 * blk), 1)
    lo = jnp.maximum(row, jnp.where(j == 0, blk, 0))
    bias = jnp.where(col >= lo, jnp.where(col <= row + blk, 0.0, MASKED), MASKED)
    first = lax.broadcasted_iota(jnp.int32, (blk, LANES), 1) < HEAD_DIM
    keep = (jnp.where(first, 1.0, 0.0).astype(BF16), jnp.where(first, 0.0, 1.0).astype(BF16))
    for pair in range(q_ref.shape[1] // LANES):
        sl = slice(pair * LANES, (pair + 1) * LANES)
        q2 = q_ref[:, sl]
        k2 = jnp.concatenate([kp_ref[:, sl], kc_ref[:, sl]], axis=0)
        v2 = jnp.concatenate([vp_ref[:, sl], vc_ref[:, sl]], axis=0)
        o_half, r_half, l_half = [], [], []
        for half in range(2):
            s = lax.dot_general(q2 * keep[half], k2, (((1,), (1,)), ((), ())),
                                preferred_element_type=F32) + bias
            m = jnp.max(s, axis=-1, keepdims=True)
            p = jnp.exp(s - m)
            den = jnp.sum(p, axis=-1, keepdims=True)
            o_half.append(jnp.dot(p.astype(BF16), v2, preferred_element_type=F32))
            r_half.append(1.0 / den)
            l_half.append(m + jnp.log(den))
        o = jnp.where(first, o_half[0], o_half[1]) * jnp.where(first, r_half[0], r_half[1])
        o_ref[:, sl] = o.astype(BF16)
        l_ref[:, sl] = jnp.where(first, l_half[0], l_half[1])


def _dilated_attention(q, k, v, group, batch, seq):
    _, dil = ATTN_PATTERNS[group]
    length = seq // dil
    nb = length // ATTN_BLOCK
    phases = min(dil, 4)
    width = phases * ATTN_WIDTH
    cur = pl.BlockSpec((None, ATTN_BLOCK, width), lambda b, p, j: (b, j, p))
    prev = pl.BlockSpec((None, ATTN_BLOCK, width), lambda b, p, j: (b, jnp.maximum(j - 1, 0), p))
    return pl.pallas_call(
        _attn_kernel,
        grid=(batch, dil // phases, nb),
        in_specs=[cur, prev, cur, prev, cur],
        out_specs=[cur, cur],
        out_shape=[jax.ShapeDtypeStruct((batch, length, dil * ATTN_WIDTH), BF16),
                   jax.ShapeDtypeStruct((batch, length, dil * ATTN_WIDTH), F32)],
        compiler_params=_params(("parallel", "parallel", "arbitrary")),
        name=f"dilated_attn_g{group}",
    )(q, k, k, v, v)


def _natural_rows(ref, stage, dil):
    if dil == 1:
        return ref[...].astype(F32)
    n = ROW_TILE // dil
    chunks = ATTN_WIDTH // LANES
    for p in range(dil):
        for c in range(chunks):
            lanes = slice(p * ATTN_WIDTH + c * LANES, p * ATTN_WIDTH + (c + 1) * LANES)
            stage[c, pl.ds(p, n, stride=dil), :] = ref[:, lanes].astype(F32)
    return jnp.concatenate([stage[c] for c in range(chunks)], axis=1)


def _merge_kernel(o0, o1, o2, l0, l1, l2, w_ref, h_ref, out_ref, so1, so2, sl1, sl2):
    dils = [d for _, d in ATTN_PATTERNS]
    a = _natural_rows(l0, None, dils[0])
    b = _natural_rows(l1, sl1, dils[1])
    c = _natural_rows(l2, sl2, dils[2])
    m = jnp.maximum(jnp.maximum(a, b), c)
    ea, eb, ec = jnp.exp(a - m), jnp.exp(b - m), jnp.exp(c - m)
    mix = ea * _natural_rows(o0, None, dils[0])
    mix += eb * _natural_rows(o1, so1, dils[1])
    mix += ec * _natural_rows(o2, so2, dils[2])
    mix = mix / (ea + eb + ec)
    out_ref[...] = h_ref[...] + jnp.dot(mix.astype(BF16), w_ref[...], preferred_element_type=F32)


def _merge_out_proj(outs, lses, w, h, seq):
    t = h.shape[0]
    tiles = seq // ROW_TILE
    slabs = [pl.BlockSpec((None, ROW_TILE // d, d * ATTN_WIDTH), lambda i: (i // tiles, i % tiles, 0))
             for _, d in ATTN_PATTERNS]
    full = pl.BlockSpec((ROW_TILE, D_MODEL), lambda i: (i, 0))
    stage = pltpu.VMEM((ATTN_WIDTH // LANES, ROW_TILE, LANES), F32)
    return pl.pallas_call(
        _merge_kernel,
        grid=(t // ROW_TILE,),
        in_specs=slabs + slabs + [pl.BlockSpec((ATTN_WIDTH, D_MODEL), lambda i: (0, 0)), full],
        out_specs=full,
        out_shape=jax.ShapeDtypeStruct((t, D_MODEL), F32),
        scratch_shapes=[stage, stage, stage, stage],
        compiler_params=_params(("parallel",)),
        name="attn_merge_out",
    )(*outs, *lses, w, h)


CLASS_ROWS = 32


def _router_kernel(x_ref, g_ref, whi_ref, wlo_ref, b_ref, tri_ref,
                   cls_ref, rank_ref, gate_ref, count_ref, run_sc):
    @pl.when(pl.program_id(0) == 0)
    def _():
        run_sc[...] = jnp.zeros_like(run_sc)

    y = _rmsnorm_rows(x_ref[...], g_ref[...])
    y_hi = y.astype(BF16)
    y_lo = (y - y_hi.astype(F32)).astype(BF16)
    nt = (((1,), (1,)), ((), ()))
    whi = whi_ref[...]
    lg = lax.dot_general(whi, y_hi, nt, preferred_element_type=F32)
    lg += lax.dot_general(whi, y_lo, nt, preferred_element_type=F32)
    lg += lax.dot_general(wlo_ref[...], y_hi, nt, preferred_element_type=F32)
    lg = lg + b_ref[...]
    row = [lg[r:r + 1, :] for r in range(N_EXPERT_GROUPS + N_EXPERTS)]

    best, grp = row[0], jnp.zeros(row[0].shape, jnp.int32)
    for g in range(1, N_EXPERT_GROUPS):
        better = row[g] > best
        grp = jnp.where(better, g, grp)
        best = jnp.where(better, row[g], best)
    gate1 = 1.0 / sum(jnp.exp(row[g] - best) for g in range(N_EXPERT_GROUPS))

    le = []
    for e in range(EXPERTS_PER_GROUP):
        v = row[N_EXPERT_GROUPS + e]
        for g in range(1, N_EXPERT_GROUPS):
            v = jnp.where(grp == g, row[N_EXPERT_GROUPS + g * EXPERTS_PER_GROUP + e], v)
        le.append(v)
    v1, i1 = le[0], jnp.zeros(grp.shape, jnp.int32)
    for e in range(1, EXPERTS_PER_GROUP):
        better = le[e] > v1
        i1 = jnp.where(better, e, i1)
        v1 = jnp.where(better, le[e], v1)
    v2, i2 = jnp.full(v1.shape, -jnp.inf, F32), jnp.zeros(grp.shape, jnp.int32)
    for e in range(EXPERTS_PER_GROUP):
        cand = jnp.where(i1 == e, -jnp.inf, le[e])
        better = cand > v2
        i2 = jnp.where(better, e, i2)
        v2 = jnp.where(better, cand, v2)
    ratio = jnp.exp(v2 - v1)
    g_first = gate1 / (1.0 + ratio)
    g_second = gate1 * ratio / (1.0 + ratio)
    first_low = i1 < i2
    lo = jnp.where(first_low, i1, i2)
    hi = jnp.where(first_low, i2, i1)
    pair = jnp.right_shift(lo * (2 * EXPERTS_PER_GROUP - 1 - lo), 1) + (hi - lo - 1)
    cls = grp * len(PAIRS) + pair
    cls_ref[...] = cls
    gate_ref[0:1, :] = jnp.where(first_low, g_first, g_second)
    gate_ref[1:2, :] = jnp.where(first_low, g_second, g_first)

    onehot = lax.broadcasted_iota(jnp.int32, (CLASS_ROWS, ROW_TILE), 0) == cls
    oh = jnp.where(onehot, 1.0, 0.0)
    earlier = jnp.dot(oh.astype(BF16), tri_ref[...], preferred_element_type=F32)
    running = run_sc[:, 0:1]
    rank_ref[...] = jnp.sum(oh * (earlier + running), axis=0, keepdims=True).astype(jnp.int32)
    run_sc[...] = run_sc[...] + jnp.sum(oh, axis=1, keepdims=True)
    count_ref[...] = run_sc[...]


def _route_tokens(h, gain, wcat, bias):
    t = h.shape[0]
    wt = wcat.T
    whi = wt.astype(BF16)
    wlo = (wt - whi.astype(F32)).astype(BF16)
    idx = jnp.arange(ROW_TILE, dtype=jnp.int32)
    tri = (idx[:, None] < idx[None, :]).astype(BF16)
    wspec = pl.BlockSpec((LANES, D_MODEL), lambda i: (0, 0))
    row_i = pl.BlockSpec((1, ROW_TILE), lambda i: (0, i))
    return pl.pallas_call(
        _router_kernel,
        grid=(t // ROW_TILE,),
        in_specs=[pl.BlockSpec((ROW_TILE, D_MODEL), lambda i: (i, 0)),
                  pl.BlockSpec((1, D_MODEL), lambda i: (0, 0)), wspec, wspec,
                  pl.BlockSpec((LANES, 1), lambda i: (0, 0)),
                  pl.BlockSpec((ROW_TILE, ROW_TILE), lambda i: (0, 0))],
        out_specs=[row_i, row_i, pl.BlockSpec((2, ROW_TILE), lambda i: (0, i)),
                   pl.BlockSpec((CLASS_ROWS, LANES), lambda i: (0, 0))],
        out_shape=[jax.ShapeDtypeStruct((1, t), jnp.int32),
                   jax.ShapeDtypeStruct((1, t), jnp.int32),
                   jax.ShapeDtypeStruct((2, t), F32),
                   jax.ShapeDtypeStruct((CLASS_ROWS, LANES), F32)],
        scratch_shapes=[pltpu.VMEM((CLASS_ROWS, LANES), F32)],
        compiler_params=_params(("arbitrary",)),
        name="router_top2",
    )(h, gain.reshape(1, D_MODEL), whi, wlo, bias, tri)


ROW_PAYLOAD = D_MODEL + LANES


def _dispatch_kernel(dest_ref, h_ref, g_ref, gate_ref, xs_init, xs_hbm, rowbuf, sem):
    del xs_init
    base = pl.program_id(0) * ROW_TILE
    rowbuf[:, :D_MODEL] = _rmsnorm_rows(h_ref[...], g_ref[...])
    rowbuf[:, D_MODEL:] = gate_ref[...]

    def row_copy(r):
        return pltpu.make_async_copy(rowbuf.at[pl.ds(r, 1), :],
                                     xs_hbm.at[pl.ds(dest_ref[base + r], 1), :], sem)

    def start(r, carry):
        row_copy(r).start()
        return carry

    def wait(r, carry):
        row_copy(r).wait()
        return carry

    lax.fori_loop(0, ROW_TILE, start, 0, unroll=8)
    lax.fori_loop(0, ROW_TILE, wait, 0, unroll=8)


def _dispatch_rows(h, gain, gate_pad, dest, rows):
    t = h.shape[0]
    grid_spec = pltpu.PrefetchScalarGridSpec(
        num_scalar_prefetch=1,
        grid=(t // ROW_TILE,),
        in_specs=[pl.BlockSpec((ROW_TILE, D_MODEL), lambda i, d: (i, 0)),
                  pl.BlockSpec((1, D_MODEL), lambda i, d: (0, 0)),
                  pl.BlockSpec((ROW_TILE, LANES), lambda i, d: (i, 0)),
                  pl.BlockSpec(memory_space=pl.ANY)],
        out_specs=pl.BlockSpec(memory_space=pl.ANY),
        scratch_shapes=[pltpu.VMEM((ROW_TILE, ROW_PAYLOAD), F32), pltpu.SemaphoreType.DMA(())],
    )
    return pl.pallas_call(
        _dispatch_kernel,
        grid_spec=grid_spec,
        out_shape=jax.ShapeDtypeStruct((rows, ROW_PAYLOAD), F32),
        input_output_aliases={4: 0},
        compiler_params=_params(("arbitrary",)),
        name="moe_dispatch",
    )(dest, h, gain.reshape(1, D_MODEL), gate_pad, jnp.zeros((rows, ROW_PAYLOAD), F32))


def _combine_kernel(dest_ref, h_ref, g_ref, ys_hbm, o_ref, buf, sem, *, final_norm):
    base = pl.program_id(0) * ROW_TILE

    def row_copy(r):
        return pltpu.make_async_copy(ys_hbm.at[pl.ds(dest_ref[base + r], 1), :],
                                     buf.at[pl.ds(r, 1), :], sem)

    def start(r, carry):
        row_copy(r).start()
        return carry

    def wait(r, carry):
        row_copy(r).wait()
        return carry

    lax.fori_loop(0, ROW_TILE, start, 0, unroll=8)
    lax.fori_loop(0, ROW_TILE, wait, 0, unroll=8)
    out = h_ref[...] + buf[...]
    o_ref[...] = _rmsnorm_rows(out, g_ref[...]) if final_norm else out


def _combine_rows(h, ys, dest, final_gain):
    t = h.shape[0]
    final_norm = final_gain is not None
    gain = final_gain if final_norm else jnp.ones((D_MODEL,), F32)
    tile = pl.BlockSpec((ROW_TILE, D_MODEL), lambda i, d: (i, 0))
    grid_spec = pltpu.PrefetchScalarGridSpec(
        num_scalar_prefetch=1,
        grid=(t // ROW_TILE,),
        in_specs=[tile, pl.BlockSpec((1, D_MODEL), lambda i, d: (0, 0)),
                  pl.BlockSpec(memory_space=pl.ANY)],
        out_specs=tile,
        scratch_shapes=[pltpu.VMEM((ROW_TILE, D_MODEL), F32), pltpu.SemaphoreType.DMA(())],
    )
    return pl.pallas_call(
        functools.partial(_combine_kernel, final_norm=final_norm),
        grid_spec=grid_spec,
        out_shape=jax.ShapeDtypeStruct((t, D_MODEL), F32),
        compiler_params=_params(("arbitrary",)),
        name="moe_combine",
    )(dest, h, gain.reshape(1, D_MODEL), ys)


def _silu(a):
    return a / (1.0 + jnp.exp(-a))


def _moe_kernel(ea_ref, eb_ref, x_ref, w1a, w3a, w2a, w1b, w3b, w2b, o_ref):
    del ea_ref, eb_ref
    x = x_ref[:, :D_MODEL].astype(BF16)
    g = x_ref[:, D_MODEL:]
    ha = _silu(jnp.dot(x, w1a[...], preferred_element_type=F32))
    ha = ha * jnp.dot(x, w3a[...], preferred_element_type=F32) * g[:, 0:1]
    hb = _silu(jnp.dot(x, w1b[...], preferred_element_type=F32))
    hb = hb * jnp.dot(x, w3b[...], preferred_element_type=F32) * g[:, 1:2]
    out = jnp.dot(ha.astype(BF16), w2a[...], preferred_element_type=F32)
    out += jnp.dot(hb.astype(BF16), w2b[...], preferred_element_type=F32)
    o_ref[...] = out


def _expert_blocks(xs, block_ea, block_eb, w1, w3, w2, layer):
    rows = xs.shape[0]
    n_blocks = rows // MOE_ROWS
    up = (None, None, D_MODEL, D_EXPERT)
    down = (None, None, D_EXPERT, D_MODEL)
    grid_spec = pltpu.PrefetchScalarGridSpec(
        num_scalar_prefetch=2,
        grid=(n_blocks,),
        in_specs=[pl.BlockSpec((MOE_ROWS, ROW_PAYLOAD), lambda i, ea, eb: (i, 0)),
                  pl.BlockSpec(up, lambda i, ea, eb: (layer, ea[i], 0, 0)),
                  pl.BlockSpec(up, lambda i, ea, eb: (layer, ea[i], 0, 0)),
                  pl.BlockSpec(down, lambda i, ea, eb: (layer, ea[i], 0, 0)),
                  pl.BlockSpec(up, lambda i, ea, eb: (layer, eb[i], 0, 0)),
                  pl.BlockSpec(up, lambda i, ea, eb: (layer, eb[i], 0, 0)),
                  pl.BlockSpec(down, lambda i, ea, eb: (layer, eb[i], 0, 0))],
        out_specs=pl.BlockSpec((MOE_ROWS, D_MODEL), lambda i, ea, eb: (i, 0)),
    )
    return pl.pallas_call(
        _moe_kernel,
        grid_spec=grid_spec,
        out_shape=jax.ShapeDtypeStruct((rows, D_MODEL), F32),
        compiler_params=_params(("arbitrary",)),
        name="expert_pair_blocks",
    )(block_ea, block_eb, xs, w1, w3, w2, w1, w3, w2)


def _hierarchical_moe(h, gain, wg, bg, we, be, w1, w3, w2, layer, final_gain=None):
    t = h.shape[0]
    pad = LANES - N_EXPERT_GROUPS - N_EXPERTS
    wcat = jnp.concatenate(
        [wg, we.transpose(1, 0, 2).reshape(D_MODEL, N_EXPERTS), jnp.zeros((D_MODEL, pad), F32)], axis=1)
    bias = jnp.concatenate([bg.astype(F32), be.astype(F32).reshape(N_EXPERTS), jnp.zeros((pad,), F32)])
    cls, rank, gates, counts = _route_tokens(h, gain, wcat, bias.reshape(LANES, 1))

    counts = counts[:N_CLASSES, 0].astype(jnp.int32)
    padded = ((counts + MOE_ROWS - 1) // MOE_ROWS) * MOE_ROWS
    pends = jnp.cumsum(padded)
    pstarts = pends - padded
    classes = jnp.arange(N_CLASSES, dtype=jnp.int32)
    cls = cls.reshape(t)
    dest = rank.reshape(t) + jnp.sum(jnp.where(cls[:, None] == classes[None, :], pstarts[None, :], 0), axis=1)
    n_blocks = -(-(t + N_CLASSES * (MOE_ROWS - 1)) // MOE_ROWS)
    rows = n_blocks * MOE_ROWS
    block_start = jnp.arange(n_blocks, dtype=jnp.int32) * MOE_ROWS
    block_cls = jnp.minimum(jnp.sum((block_start[:, None] >= pends[None, :]).astype(jnp.int32), axis=1),
                            N_CLASSES - 1)
    pair_lo = jnp.asarray([p[0] for p in PAIRS], jnp.int32)
    pair_hi = jnp.asarray([p[1] for p in PAIRS], jnp.int32)
    grp = block_cls // len(PAIRS)
    block_ea = grp * EXPERTS_PER_GROUP + pair_lo[block_cls % len(PAIRS)]
    block_eb = grp * EXPERTS_PER_GROUP + pair_hi[block_cls % len(PAIRS)]

    gate_pad = jnp.pad(gates.T, ((0, 0), (0, LANES - gates.shape[0])))
    xs = _dispatch_rows(h, gain, gate_pad, dest, rows)
    ys = _expert_blocks(xs, block_ea, block_eb, w1, w3, w2, layer)
    return _combine_rows(h, ys, dest, final_gain)


def _norm_proj_kernel(x_ref, g_ref, w_ref, o_ref):
    y = _rmsnorm_rows(x_ref[...], g_ref[...]).astype(BF16)
    o_ref[...] = jnp.dot(y, w_ref[...], preferred_element_type=F32)


def _norm_proj(x, gain, w):
    t, n = x.shape[0], w.shape[1]
    return pl.pallas_call(
        _norm_proj_kernel,
        grid=(t // ROW_TILE,),
        in_specs=[pl.BlockSpec((ROW_TILE, D_MODEL), lambda i: (i, 0)),
                  pl.BlockSpec((1, D_MODEL), lambda i: (0, 0)),
                  pl.BlockSpec((D_MODEL, n), lambda i: (0, 0))],
        out_specs=pl.BlockSpec((ROW_TILE, n), lambda i: (i, 0)),
        out_shape=jax.ShapeDtypeStruct((t, n), F32),
        compiler_params=_params(("parallel",)),
        name="norm_proj",
    )(x, gain.reshape(1, D_MODEL), w)


def _proj_residual_kernel(z_ref, w_ref, h_ref, o_ref):
    o_ref[...] = h_ref[...] + jnp.dot(z_ref[...], w_ref[...], preferred_element_type=F32)


def _proj_residual(z, w, h):
    t, n = h.shape
    k = z.shape[1]
    return pl.pallas_call(
        _proj_residual_kernel,
        grid=(t // ROW_TILE,),
        in_specs=[pl.BlockSpec((ROW_TILE, k), lambda i: (i, 0)),
                  pl.BlockSpec((k, n), lambda i: (0, 0)),
                  pl.BlockSpec((ROW_TILE, n), lambda i: (i, 0))],
        out_specs=pl.BlockSpec((ROW_TILE, n), lambda i: (i, 0)),
        out_shape=jax.ShapeDtypeStruct((t, n), F32),
        compiler_params=_params(("parallel",)),
        name="proj_residual",
    )(z, w, h)


def _pool_kernel(u_ref, wg_ref, sc_ref, z_ref):
    seq = u_ref.shape[0]
    row = lax.broadcasted_iota(jnp.int32, (seq, POOL_GROUP_DIM), 0)
    for g, window in enumerate(POOL_WINDOWS):
        cols = slice(g * POOL_GROUP_DIM, (g + 1) * POOL_GROUP_DIM)
        u = u_ref[:, cols]
        s = u
        step = 1
        while step < window:
            s = s + jnp.where(row >= step, pltpu.roll(s, step, 0), 0.0)
            step *= 2
        count = jnp.minimum(row + 1, window).astype(F32)
        pooled = (s / count - u).astype(BF16)
        z = jnp.dot(pooled, wg_ref[g], preferred_element_type=F32) * sc_ref[:, cols]
        z_ref[:, cols] = z.astype(BF16)


def _pool_mixer_inner(u, w_group, scale, batch, seq):
    blk = pl.BlockSpec((None, seq, D_MODEL), lambda b: (b, 0, 0))
    z = pl.pallas_call(
        _pool_kernel,
        grid=(batch,),
        in_specs=[blk,
                  pl.BlockSpec((N_POOL_GROUPS, POOL_GROUP_DIM, POOL_GROUP_DIM), lambda b: (0, 0, 0)),
                  pl.BlockSpec((1, D_MODEL), lambda b: (0, 0))],
        out_specs=blk,
        out_shape=jax.ShapeDtypeStruct((batch, seq, D_MODEL), BF16),
        compiler_params=_params(("parallel",)),
        name="causal_pool",
    )(u.reshape(batch, seq, D_MODEL), w_group, scale.reshape(1, D_MODEL))
    return z.reshape(batch * seq, D_MODEL)


def kernel(x, positions, norm_mix, norm_ffn, norm_final, attn_w_in, attn_w_out, pool_w_in, pool_w_group, pool_scale, pool_w_out, router_group_w, router_group_b, router_expert_w, router_expert_b, expert_w1, expert_w3, expert_w2):
    batch, seq, _ = x.shape
    t = batch * seq
    h = x.reshape(t, D_MODEL)
    w1 = expert_w1.astype(BF16)
    w3 = expert_w3.astype(BF16)
    w2 = expert_w2.astype(BF16)

    c, s1, s2 = _rope_tables(positions)
    qkv = _qkv_proj(h, norm_mix[0], attn_w_in[0].astype(BF16), c, s1, s2, batch, seq)
    outs, lses = [], []
    for g in range(N_ATTN_GROUPS):
        o, lse = _dilated_attention(qkv[3 * g], qkv[3 * g + 1], qkv[3 * g + 2], g, batch, seq)
        outs.append(o)
        lses.append(lse)
    h = _merge_out_proj(outs, lses, attn_w_out[0].astype(BF16), h, seq)
    h = _hierarchical_moe(h, norm_ffn[0], router_group_w[0], router_group_b[0],
                          router_expert_w[0], router_expert_b[0], w1, w3, w2, 0)

    u = _norm_proj(h, norm_mix[1], pool_w_in[0].astype(BF16))
    z = _pool_mixer_inner(u, pool_w_group[0].astype(BF16), pool_scale[0], batch, seq)
    h = _proj_residual(z, pool_w_out[0].astype(BF16), h)
    out = _hierarchical_moe(h, norm_ffn[1], router_group_w[1], router_group_b[1],
                            router_expert_w[1], router_expert_b[1], w1, w3, w2, 1,
                            final_gain=norm_final)
    return out.reshape(batch, seq, D_MODEL)
```

```python
import functools
import math

import jax
import jax.numpy as jnp
from jax import lax
from jax.experimental import pallas as pl
from jax.experimental.pallas import tpu as pltpu

F32 = jnp.float32
BF16 = jnp.bfloat16

D_MODEL = 1024
ATTN_PATTERNS = ((128, 1), (512, 4), (2048, 16))
N_ATTN_GROUPS = len(ATTN_PATTERNS)
HEADS = 8
HEAD_DIM = 64
ATTN_WIDTH = HEADS * HEAD_DIM
QKV_WIDTH = N_ATTN_GROUPS * 3 * ATTN_WIDTH
ROT_DIM = HEAD_DIM // 4
ROPE_THETA = 500000.0
ATTN_BLOCK = 128
POOL_WINDOWS = (2, 4, 8, 16)
N_POOL_GROUPS = len(POOL_WINDOWS)
POOL_GROUP_DIM = D_MODEL // N_POOL_GROUPS
N_EXPERT_GROUPS = 4
EXPERTS_PER_GROUP = 4
N_EXPERTS = N_EXPERT_GROUPS * EXPERTS_PER_GROUP
D_EXPERT = D_MODEL // 2
RMS_EPS = 1e-6

PAIRS = ((0, 1), (0, 2), (0, 3), (1, 2), (1, 3), (2, 3))
N_CLASSES = N_EXPERT_GROUPS * len(PAIRS)

ROW_TILE = 512
COL_TILE = 512
MOE_ROWS = 256
LANES = 128
VMEM_LIMIT = 48 * 1024 * 1024
MASKED = -1e30


def _params(sem):
    return pltpu.CompilerParams(dimension_semantics=sem, vmem_limit_bytes=VMEM_LIMIT)


def _rmsnorm_rows(x, g):
    ms = jnp.mean(x * x, axis=-1, keepdims=True)
    return (x * lax.rsqrt(ms + RMS_EPS)) * g


def _rope_kernel(pos_ref, freq_ref, c_ref, s1_ref, s2_ref):
    ang = pos_ref[...].astype(F32) * freq_ref[...]
    lane = lax.broadcasted_iota(jnp.int32, ang.shape, 1) % HEAD_DIM
    sin = jnp.sin(ang)
    c_ref[...] = jnp.cos(ang)
    s1_ref[...] = jnp.where(lane < ROT_DIM // 2, -sin, 0.0)
    s2_ref[...] = jnp.where(lane >= ROT_DIM // 2, sin, 0.0)


def _rope_tables(positions):
    t = positions.size
    half = ROT_DIM // 2
    inv = [ROPE_THETA ** (-(i * 2.0 / ROT_DIM)) for i in range(half)]
    per_head = inv + inv + [0.0] * (HEAD_DIM - ROT_DIM)
    freq = jnp.asarray([per_head * (LANES // HEAD_DIM)], F32)
    tm = 1024
    out = jax.ShapeDtypeStruct((t, LANES), F32)
    spec = pl.BlockSpec((tm, LANES), lambda i: (i, 0))
    return pl.pallas_call(
        _rope_kernel,
        grid=(t // tm,),
        in_specs=[pl.BlockSpec((tm, 1), lambda i: (i, 0)),
                  pl.BlockSpec((1, LANES), lambda i: (0, 0))],
        out_specs=[spec, spec, spec],
        out_shape=[out, out, out],
        compiler_params=_params(("parallel",)),
        name="rope_tables",
    )(positions.reshape(t, 1), freq)


def _qkv_kernel(x_ref, g_ref, w_ref, c_ref, s1_ref, s2_ref, *rest):
    o_refs, (y_sc, acc_sc) = rest[:-2], rest[-2:]
    y_sc[...] = _rmsnorm_rows(x_ref[...], g_ref[...]).astype(BF16)
    chunks = COL_TILE // LANES
    for col in range(QKV_WIDTH // COL_TILE):
        dil = ATTN_PATTERNS[col // 3][1]
        n = ROW_TILE // dil
        buf = col % 2
        acc = jnp.dot(y_sc[...], w_ref[:, col * COL_TILE:(col + 1) * COL_TILE],
                      preferred_element_type=F32)
        for c in range(chunks):
            acc_sc[buf, c] = acc[:, c * LANES:(c + 1) * LANES]
        rotary = col % 3 != 2
        for p in range(dil):
            rows = pl.ds(p, n, stride=dil) if dil > 1 else slice(None)
            if rotary:
                cos, sin_up, sin_dn = c_ref[rows, :], s1_ref[rows, :], s2_ref[rows, :]
            for c in range(chunks):
                a = acc_sc[buf, c, rows, :]
                if rotary:
                    a = (a * cos + pltpu.roll(a, LANES - ROT_DIM // 2, 1) * sin_up
                         + pltpu.roll(a, ROT_DIM // 2, 1) * sin_dn)
                if col % 3 == 0:
                    a = a * (1.0 / math.sqrt(HEAD_DIM))
                lanes = slice(p * COL_TILE + c * LANES, p * COL_TILE + (c + 1) * LANES)
                o_refs[col][:, lanes] = a.astype(BF16)


def _qkv_proj(x, gain, w, c, s1, s2, batch, seq):
    t = x.shape[0]
    tiles = seq // ROW_TILE
    tab = pl.BlockSpec((ROW_TILE, LANES), lambda i: (i, 0))
    out_specs, out_shapes = [], []
    for _, dil in ATTN_PATTERNS:
        for _ in range(3):
            out_specs.append(pl.BlockSpec((None, ROW_TILE // dil, dil * ATTN_WIDTH),
                                          lambda i: (i // tiles, i % tiles, 0)))
            out_shapes.append(jax.ShapeDtypeStruct((batch, seq // dil, dil * ATTN_WIDTH), BF16))
    return pl.pallas_call(
        _qkv_kernel,
        grid=(t // ROW_TILE,),
        in_specs=[pl.BlockSpec((ROW_TILE, D_MODEL), lambda i: (i, 0)),
                  pl.BlockSpec((1, D_MODEL), lambda i: (0, 0)),
                  pl.BlockSpec((D_MODEL, QKV_WIDTH), lambda i: (0, 0)),
                  tab, tab, tab],
        out_specs=out_specs,
        out_shape=out_shapes,
        scratch_shapes=[pltpu.VMEM((ROW_TILE, D_MODEL), BF16),
                        pltpu.VMEM((2, COL_TILE // LANES, ROW_TILE, LANES), F32)],
        compiler_params=_params(("parallel",)),
        name="qkv_proj",
    )(x, gain.reshape(1, D_MODEL), w, c, s1, s2)


def _attn_kernel(q_ref, kp_ref, kc_ref, vp_ref, vc_ref, o_ref, l_ref):
    j = pl.program_id(2)
    blk = ATTN_BLOCK
    row = lax.broadcasted_iota(jnp.int32, (blk, 2 * blk), 0)
    col = lax.broadcasted_iota(jnp.int32, (blk, 2 * blk), 1)
    band = jnp.where(col >= row, jnp.where(col <= row + blk, 0.0, MASKED), MASKED)
    lo = jnp.maximum(row, jnp.where(j == 0, blk, 0))
    band_first = jnp.where(col >= lo, band, MASKED)
    first = lax.broadcasted_iota(jnp.int32, (blk, LANES), 1) < HEAD_DIM
    keep = (jnp.where(first, 1.0, 0.0).astype(BF16), jnp.where(first, 0.0, 1.0).astype(BF16))
    for sub in range(q_ref.shape[0] // blk):
        rows = slice(sub * blk, (sub + 1) * blk)
        before = slice((sub - 1) * blk, sub * blk)
        bias = band_first if sub == 0 else band
        for pair in range(q_ref.shape[1] // LANES):
            sl = slice(pair * LANES, (pair + 1) * LANES)
            q2 = q_ref[rows, sl]
            k_prev = kp_ref[:, sl] if sub == 0 else kc_ref[before, sl]
            v_prev = vp_ref[:, sl] if sub == 0 else vc_ref[before, sl]
            k2 = jnp.concatenate([k_prev, kc_ref[rows, sl]], axis=0)
            v2 = jnp.concatenate([v_prev, vc_ref[rows, sl]], axis=0)
            o_half, r_half, l_half = [], [], []
            for half in range(2):
                s = lax.dot_general(q2 * keep[half], k2, (((1,), (1,)), ((), ())),
                                    preferred_element_type=F32) + bias
                m = jnp.max(s, axis=-1, keepdims=True)
                p = jnp.exp(s - m)
                den = jnp.sum(p, axis=-1, keepdims=True)
                o_half.append(jnp.dot(p.astype(BF16), v2, preferred_element_type=F32))
                r_half.append(1.0 / den)
                l_half.append(m + jnp.log(den))
            o = jnp.where(first, o_half[0], o_half[1]) * jnp.where(first, r_half[0], r_half[1])
            o_ref[rows, sl] = o.astype(BF16)
            l_ref[rows, sl] = jnp.where(first, l_half[0], l_half[1])


def _dilated_attention(q, k, v, group, batch, seq):
    _, dil = ATTN_PATTERNS[group]
    length = seq // dil
    nb = length // ATTN_BLOCK
    units = 4
    phases = min(dil, units)
    blocks = units // phases
    width = phases * ATTN_WIDTH
    cur = pl.BlockSpec((None, blocks * ATTN_BLOCK, width), lambda b, p, j: (b, j, p))
    prev = pl.BlockSpec((None, ATTN_BLOCK, width),
                        lambda b, p, j: (b, jnp.maximum(blocks * j - 1, 0), p))
    return pl.pallas_call(
        _attn_kernel,
        grid=(batch, dil // phases, nb // blocks),
        in_specs=[cur, prev, cur, prev, cur],
        out_specs=[cur, cur],
        out_shape=[jax.ShapeDtypeStruct((batch, length, dil * ATTN_WIDTH), BF16),
                   jax.ShapeDtypeStruct((batch, length, dil * ATTN_WIDTH), F32)],
        compiler_params=_params(("parallel", "parallel", "arbitrary")),
        name=f"dilated_attn_g{group}",
    )(q, k, k, v, v)


def _natural_rows(ref, stage, dil):
    if dil == 1:
        return ref[...].astype(F32)
    n = ROW_TILE // dil
    chunks = ATTN_WIDTH // LANES
    for p in range(dil):
        for c in range(chunks):
            lanes = slice(p * ATTN_WIDTH + c * LANES, p * ATTN_WIDTH + (c + 1) * LANES)
            stage[c, pl.ds(p, n, stride=dil), :] = ref[:, lanes].astype(F32)
    return jnp.concatenate([stage[c] for c in range(chunks)], axis=1)


def _merge_kernel(o0, o1, o2, l0, l1, l2, w_ref, h_ref, out_ref, so1, so2, sl1, sl2):
    dils = [d for _, d in ATTN_PATTERNS]
    a = _natural_rows(l0, None, dils[0])
    b = _natural_rows(l1, sl1, dils[1])
    c = _natural_rows(l2, sl2, dils[2])
    m = jnp.maximum(jnp.maximum(a, b), c)
    ea, eb, ec = jnp.exp(a - m), jnp.exp(b - m), jnp.exp(c - m)
    mix = ea * _natural_rows(o0, None, dils[0])
    mix += eb * _natural_rows(o1, so1, dils[1])
    mix += ec * _natural_rows(o2, so2, dils[2])
    mix = mix / (ea + eb + ec)
    out_ref[...] = h_ref[...] + jnp.dot(mix.astype(BF16), w_ref[...], preferred_element_type=F32)


def _merge_out_proj(outs, lses, w, h, seq):
    t = h.shape[0]
    tiles = seq // ROW_TILE
    slabs = [pl.BlockSpec((None, ROW_TILE // d, d * ATTN_WIDTH), lambda i: (i // tiles, i % tiles, 0))
             for _, d in ATTN_PATTERNS]
    full = pl.BlockSpec((ROW_TILE, D_MODEL), lambda i: (i, 0))
    stage = pltpu.VMEM((ATTN_WIDTH // LANES, ROW_TILE, LANES), F32)
    return pl.pallas_call(
        _merge_kernel,
        grid=(t // ROW_TILE,),
        in_specs=slabs + slabs + [pl.BlockSpec((ATTN_WIDTH, D_MODEL), lambda i: (0, 0)), full],
        out_specs=full,
        out_shape=jax.ShapeDtypeStruct((t, D_MODEL), F32),
        scratch_shapes=[stage, stage, stage, stage],
        compiler_params=_params(("parallel",)),
        name="attn_merge_out",
    )(*outs, *lses, w, h)


CLASS_ROWS = 32


def _router_kernel(x_ref, g_ref, whi_ref, wlo_ref, b_ref, tri_ref,
                   cls_ref, rank_ref, gate_ref, count_ref, run_sc):
    @pl.when(pl.program_id(0) == 0)
    def _():
        run_sc[...] = jnp.zeros_like(run_sc)

    y = _rmsnorm_rows(x_ref[...], g_ref[...])
    y_hi = y.astype(BF16)
    y_lo = (y - y_hi.astype(F32)).astype(BF16)
    nt = (((1,), (1,)), ((), ()))
    whi = whi_ref[...]
    lg = lax.dot_general(whi, y_hi, nt, preferred_element_type=F32)
    lg += lax.dot_general(whi, y_lo, nt, preferred_element_type=F32)
    lg += lax.dot_general(wlo_ref[...], y_hi, nt, preferred_element_type=F32)
    lg = lg + b_ref[...]
    row = [lg[r:r + 1, :] for r in range(N_EXPERT_GROUPS + N_EXPERTS)]

    best, grp = row[0], jnp.zeros(row[0].shape, jnp.int32)
    for g in range(1, N_EXPERT_GROUPS):
        better = row[g] > best
        grp = jnp.where(better, g, grp)
        best = jnp.where(better, row[g], best)
    gate1 = 1.0 / sum(jnp.exp(row[g] - best) for g in range(N_EXPERT_GROUPS))

    le = []
    for e in range(EXPERTS_PER_GROUP):
        v = row[N_EXPERT_GROUPS + e]
        for g in range(1, N_EXPERT_GROUPS):
            v = jnp.where(grp == g, row[N_EXPERT_GROUPS + g * EXPERTS_PER_GROUP + e], v)
        le.append(v)
    v1, i1 = le[0], jnp.zeros(grp.shape, jnp.int32)
    for e in range(1, EXPERTS_PER_GROUP):
        better = le[e] > v1
        i1 = jnp.where(better, e, i1)
        v1 = jnp.where(better, le[e], v1)
    v2, i2 = jnp.full(v1.shape, -jnp.inf, F32), jnp.zeros(grp.shape, jnp.int32)
    for e in range(EXPERTS_PER_GROUP):
        cand = jnp.where(i1 == e, -jnp.inf, le[e])
        better = cand > v2
        i2 = jnp.where(better, e, i2)
        v2 = jnp.where(better, cand, v2)
    ratio = jnp.exp(v2 - v1)
    g_first = gate1 / (1.0 + ratio)
    g_second = gate1 * ratio / (1.0 + ratio)
    first_low = i1 < i2
    lo = jnp.where(first_low, i1, i2)
    hi = jnp.where(first_low, i2, i1)
    pair = jnp.right_shift(lo * (2 * EXPERTS_PER_GROUP - 1 - lo), 1) + (hi - lo - 1)
    cls = grp * len(PAIRS) + pair
    cls_ref[...] = cls
    gate_ref[0:1, :] = jnp.where(first_low, g_first, g_second)
    gate_ref[1:2, :] = jnp.where(first_low, g_second, g_first)

    onehot = lax.broadcasted_iota(jnp.int32, (CLASS_ROWS, ROW_TILE), 0) == cls
    oh = jnp.where(onehot, 1.0, 0.0)
    earlier = jnp.dot(oh.astype(BF16), tri_ref[...], preferred_element_type=F32)
    running = run_sc[:, 0:1]
    rank_ref[...] = jnp.sum(oh * (earlier + running), axis=0, keepdims=True).astype(jnp.int32)
    run_sc[...] = run_sc[...] + jnp.sum(oh, axis=1, keepdims=True)
    count_ref[...] = run_sc[...]


def _route_tokens(h, gain, wcat, bias):
    t = h.shape[0]
    wt = wcat.T
    whi = wt.astype(BF16)
    wlo = (wt - whi.astype(F32)).astype(BF16)
    idx = jnp.arange(ROW_TILE, dtype=jnp.int32)
    tri = (idx[:, None] < idx[None, :]).astype(BF16)
    wspec = pl.BlockSpec((LANES, D_MODEL), lambda i: (0, 0))
    row_i = pl.BlockSpec((1, ROW_TILE), lambda i: (0, i))
    return pl.pallas_call(
        _router_kernel,
        grid=(t // ROW_TILE,),
        in_specs=[pl.BlockSpec((ROW_TILE, D_MODEL), lambda i: (i, 0)),
                  pl.BlockSpec((1, D_MODEL), lambda i: (0, 0)), wspec, wspec,
                  pl.BlockSpec((LANES, 1), lambda i: (0, 0)),
                  pl.BlockSpec((ROW_TILE, ROW_TILE), lambda i: (0, 0))],
        out_specs=[row_i, row_i, pl.BlockSpec((2, ROW_TILE), lambda i: (0, i)),
                   pl.BlockSpec((CLASS_ROWS, LANES), lambda i: (0, 0))],
        out_shape=[jax.ShapeDtypeStruct((1, t), jnp.int32),
                   jax.ShapeDtypeStruct((1, t), jnp.int32),
                   jax.ShapeDtypeStruct((2, t), F32),
                   jax.ShapeDtypeStruct((CLASS_ROWS, LANES), F32)],
        scratch_shapes=[pltpu.VMEM((CLASS_ROWS, LANES), F32)],
        compiler_params=_params(("arbitrary",)),
        name="router_top2",
    )(h, gain.reshape(1, D_MODEL), whi, wlo, bias, tri)


ROW_PAYLOAD = D_MODEL + LANES


def _dispatch_kernel(dest_ref, h_ref, g_ref, gate_ref, xs_init, xs_hbm, rowbuf, sem):
    del xs_init
    base = pl.program_id(0) * ROW_TILE
    rowbuf[:, :D_MODEL] = _rmsnorm_rows(h_ref[...], g_ref[...])
    rowbuf[:, D_MODEL:] = gate_ref[...]

    def row_copy(r):
        return pltpu.make_async_copy(rowbuf.at[pl.ds(r, 1), :],
                                     xs_hbm.at[pl.ds(dest_ref[base + r], 1), :], sem)

    def start(r, carry):
        row_copy(r).start()
        return carry

    def wait(r, carry):
        row_copy(r).wait()
        return carry

    lax.fori_loop(0, ROW_TILE, start, 0, unroll=8)
    lax.fori_loop(0, ROW_TILE, wait, 0, unroll=8)


def _dispatch_rows(h, gain, gate_pad, dest, rows):
    t = h.shape[0]
    grid_spec = pltpu.PrefetchScalarGridSpec(
        num_scalar_prefetch=1,
        grid=(t // ROW_TILE,),
        in_specs=[pl.BlockSpec((ROW_TILE, D_MODEL), lambda i, d: (i, 0)),
                  pl.BlockSpec((1, D_MODEL), lambda i, d: (0, 0)),
                  pl.BlockSpec((ROW_TILE, LANES), lambda i, d: (i, 0)),
                  pl.BlockSpec(memory_space=pl.ANY)],
        out_specs=pl.BlockSpec(memory_space=pl.ANY),
        scratch_shapes=[pltpu.VMEM((ROW_TILE, ROW_PAYLOAD), F32), pltpu.SemaphoreType.DMA(())],
    )
    return pl.pallas_call(
        _dispatch_kernel,
        grid_spec=grid_spec,
        out_shape=jax.ShapeDtypeStruct((rows, ROW_PAYLOAD), F32),
        input_output_aliases={4: 0},
        compiler_params=_params(("arbitrary",)),
        name="moe_dispatch",
    )(dest, h, gain.reshape(1, D_MODEL), gate_pad, jnp.zeros((rows, ROW_PAYLOAD), F32))


def _combine_kernel(dest_ref, h_ref, g_ref, ys_hbm, o_ref, buf, sem, *, final_norm):
    base = pl.program_id(0) * ROW_TILE

    def row_copy(r):
        return pltpu.make_async_copy(ys_hbm.at[pl.ds(dest_ref[base + r], 1), :],
                                     buf.at[pl.ds(r, 1), :], sem)

    def start(r, carry):
        row_copy(r).start()
        return carry

    def wait(r, carry):
        row_copy(r).wait()
        return carry

    lax.fori_loop(0, ROW_TILE, start, 0, unroll=8)
    lax.fori_loop(0, ROW_TILE, wait, 0, unroll=8)
    out = h_ref[...] + buf[...]
    o_ref[...] = _rmsnorm_rows(out, g_ref[...]) if final_norm else out


def _combine_rows(h, ys, dest, final_gain):
    t = h.shape[0]
    final_norm = final_gain is not None
    gain = final_gain if final_norm else jnp.ones((D_MODEL,), F32)
    tile = pl.BlockSpec((ROW_TILE, D_MODEL), lambda i, d: (i, 0))
    grid_spec = pltpu.PrefetchScalarGridSpec(
        num_scalar_prefetch=1,
        grid=(t // ROW_TILE,),
        in_specs=[tile, pl.BlockSpec((1, D_MODEL), lambda i, d: (0, 0)),
                  pl.BlockSpec(memory_space=pl.ANY)],
        out_specs=tile,
        scratch_shapes=[pltpu.VMEM((ROW_TILE, D_MODEL), F32), pltpu.SemaphoreType.DMA(())],
    )
    return pl.pallas_call(
        functools.partial(_combine_kernel, final_norm=final_norm),
        grid_spec=grid_spec,
        out_shape=jax.ShapeDtypeStruct((t, D_MODEL), F32),
        compiler_params=_params(("arbitrary",)),
        name="moe_combine",
    )(dest, h, gain.reshape(1, D_MODEL), ys)


def _silu(a):
    return a / (1.0 + jnp.exp(-a))


def _moe_kernel(ea_ref, eb_ref, x_ref, w1a, w3a, w2a, w1b, w3b, w2b, o_ref):
    del ea_ref, eb_ref
    x = x_ref[:, :D_MODEL].astype(BF16)
    g = x_ref[:, D_MODEL:]
    ha = _silu(jnp.dot(x, w1a[...], preferred_element_type=F32))
    ha = ha * jnp.dot(x, w3a[...], preferred_element_type=F32) * g[:, 0:1]
    hb = _silu(jnp.dot(x, w1b[...], preferred_element_type=F32))
    hb = hb * jnp.dot(x, w3b[...], preferred_element_type=F32) * g[:, 1:2]
    out = jnp.dot(ha.astype(BF16), w2a[...], preferred_element_type=F32)
    out += jnp.dot(hb.astype(BF16), w2b[...], preferred_element_type=F32)
    o_ref[...] = out


def _expert_blocks(xs, block_ea, block_eb, w1, w3, w2, layer):
    rows = xs.shape[0]
    n_blocks = rows // MOE_ROWS
    up = (None, None, D_MODEL, D_EXPERT)
    down = (None, None, D_EXPERT, D_MODEL)
    grid_spec = pltpu.PrefetchScalarGridSpec(
        num_scalar_prefetch=2,
        grid=(n_blocks,),
        in_specs=[pl.BlockSpec((MOE_ROWS, ROW_PAYLOAD), lambda i, ea, eb: (i, 0)),
                  pl.BlockSpec(up, lambda i, ea, eb: (layer, ea[i], 0, 0)),
                  pl.BlockSpec(up, lambda i, ea, eb: (layer, ea[i], 0, 0)),
                  pl.BlockSpec(down, lambda i, ea, eb: (layer, ea[i], 0, 0)),
                  pl.BlockSpec(up, lambda i, ea, eb: (layer, eb[i], 0, 0)),
                  pl.BlockSpec(up, lambda i, ea, eb: (layer, eb[i], 0, 0)),
                  pl.BlockSpec(down, lambda i, ea, eb: (layer, eb[i], 0, 0))],
        out_specs=pl.BlockSpec((MOE_ROWS, D_MODEL), lambda i, ea, eb: (i, 0)),
    )
    return pl.pallas_call(
        _moe_kernel,
        grid_spec=grid_spec,
        out_shape=jax.ShapeDtypeStruct((rows, D_MODEL), F32),
        compiler_params=_params(("arbitrary",)),
        name="expert_pair_blocks",
    )(block_ea, block_eb, xs, w1, w3, w2, w1, w3, w2)


def _hierarchical_moe(h, gain, wg, bg, we, be, w1, w3, w2, layer, final_gain=None):
    t = h.shape[0]
    pad = LANES - N_EXPERT_GROUPS - N_EXPERTS
    wcat = jnp.concatenate(
        [wg, we.transpose(1, 0, 2).reshape(D_MODEL, N_EXPERTS), jnp.zeros((D_MODEL, pad), F32)], axis=1)
    bias = jnp.concatenate([bg.astype(F32), be.astype(F32).reshape(N_EXPERTS), jnp.zeros((pad,), F32)])
    cls, rank, gates, counts = _route_tokens(h, gain, wcat, bias.reshape(LANES, 1))

    counts = counts[:N_CLASSES, 0].astype(jnp.int32)
    padded = ((counts + MOE_ROWS - 1) // MOE_ROWS) * MOE_ROWS
    pends = jnp.cumsum(padded)
    pstarts = pends - padded
    classes = jnp.arange(N_CLASSES, dtype=jnp.int32)
    cls = cls.reshape(t)
    dest = rank.reshape(t) + jnp.sum(jnp.where(cls[:, None] == classes[None, :], pstarts[None, :], 0), axis=1)
    n_blocks = -(-(t + N_CLASSES * (MOE_ROWS - 1)) // MOE_ROWS)
    rows = n_blocks * MOE_ROWS
    block_start = jnp.arange(n_blocks, dtype=jnp.int32) * MOE_ROWS
    block_cls = jnp.minimum(jnp.sum((block_start[:, None] >= pends[None, :]).astype(jnp.int32), axis=1),
                            N_CLASSES - 1)
    pair_lo = jnp.asarray([p[0] for p in PAIRS], jnp.int32)
    pair_hi = jnp.asarray([p[1] for p in PAIRS], jnp.int32)
    grp = block_cls // len(PAIRS)
    block_ea = grp * EXPERTS_PER_GROUP + pair_lo[block_cls % len(PAIRS)]
    block_eb = grp * EXPERTS_PER_GROUP + pair_hi[block_cls % len(PAIRS)]

    gate_pad = jnp.pad(gates.T, ((0, 0), (0, LANES - gates.shape[0])))
    xs = _dispatch_rows(h, gain, gate_pad, dest, rows)
    ys = _expert_blocks(xs, block_ea, block_eb, w1, w3, w2, layer)
    return _combine_rows(h, ys, dest, final_gain)


def _norm_proj_kernel(x_ref, g_ref, w_ref, o_ref):
    y = _rmsnorm_rows(x_ref[...], g_ref[...]).astype(BF16)
    o_ref[...] = jnp.dot(y, w_ref[...], preferred_element_type=F32)


def _norm_proj(x, gain, w):
    t, n = x.shape[0], w.shape[1]
    return pl.pallas_call(
        _norm_proj_kernel,
        grid=(t // ROW_TILE,),
        in_specs=[pl.BlockSpec((ROW_TILE, D_MODEL), lambda i: (i, 0)),
                  pl.BlockSpec((1, D_MODEL), lambda i: (0, 0)),
                  pl.BlockSpec((D_MODEL, n), lambda i: (0, 0))],
        out_specs=pl.BlockSpec((ROW_TILE, n), lambda i: (i, 0)),
        out_shape=jax.ShapeDtypeStruct((t, n), F32),
        compiler_params=_params(("parallel",)),
        name="norm_proj",
    )(x, gain.reshape(1, D_MODEL), w)


def _proj_residual_kernel(z_ref, w_ref, h_ref, o_ref):
    o_ref[...] = h_ref[...] + jnp.dot(z_ref[...], w_ref[...], preferred_element_type=F32)


def _proj_residual(z, w, h):
    t, n = h.shape
    k = z.shape[1]
    return pl.pallas_call(
        _proj_residual_kernel,
        grid=(t // ROW_TILE,),
        in_specs=[pl.BlockSpec((ROW_TILE, k), lambda i: (i, 0)),
                  pl.BlockSpec((k, n), lambda i: (0, 0)),
                  pl.BlockSpec((ROW_TILE, n), lambda i: (i, 0))],
        out_specs=pl.BlockSpec((ROW_TILE, n), lambda i: (i, 0)),
        out_shape=jax.ShapeDtypeStruct((t, n), F32),
        compiler_params=_params(("parallel",)),
        name="proj_residual",
    )(z, w, h)


def _pool_kernel(u_ref, wg_ref, sc_ref, z_ref):
    seq = u_ref.shape[0]
    row = lax.broadcasted_iota(jnp.int32, (seq, POOL_GROUP_DIM), 0)
    for g, window in enumerate(POOL_WINDOWS):
        cols = slice(g * POOL_GROUP_DIM, (g + 1) * POOL_GROUP_DIM)
        u = u_ref[:, cols]
        s = u
        step = 1
        while step < window:
            s = s + jnp.where(row >= step, pltpu.roll(s, step, 0), 0.0)
            step *= 2
        count = jnp.minimum(row + 1, window).astype(F32)
        pooled = (s / count - u).astype(BF16)
        z = jnp.dot(pooled, wg_ref[g], preferred_element_type=F32) * sc_ref[:, cols]
        z_ref[:, cols] = z.astype(BF16)


def _pool_mixer_inner(u, w_group, scale, batch, seq):
    blk = pl.BlockSpec((None, seq, D_MODEL), lambda b: (b, 0, 0))
    z = pl.pallas_call(
        _pool_kernel,
        grid=(batch,),
        in_specs=[blk,
                  pl.BlockSpec((N_POOL_GROUPS, POOL_GROUP_DIM, POOL_GROUP_DIM), lambda b: (0, 0, 0)),
                  pl.BlockSpec((1, D_MODEL), lambda b: (0, 0))],
        out_specs=blk,
        out_shape=jax.ShapeDtypeStruct((batch, seq, D_MODEL), BF16),
        compiler_params=_params(("parallel",)),
        name="causal_pool",
    )(u.reshape(batch, seq, D_MODEL), w_group, scale.reshape(1, D_MODEL))
    return z.reshape(batch * seq, D_MODEL)


def kernel(x, positions, norm_mix, norm_ffn, norm_final, attn_w_in, attn_w_out, pool_w_in, pool_w_group, pool_scale, pool_w_out, router_group_w, router_group_b, router_expert_w, router_expert_b, expert_w1, expert_w3, expert_w2):
    batch, seq, _ = x.shape
    t = batch * seq
    h = x.reshape(t, D_MODEL)
    w1 = expert_w1.astype(BF16)
    w3 = expert_w3.astype(BF16)
    w2 = expert_w2.astype(BF16)

    c, s1, s2 = _rope_tables(positions)
    qkv = _qkv_proj(h, norm_mix[0], attn_w_in[0].astype(BF16), c, s1, s2, batch, seq)
    outs, lses = [], []
    for g in range(N_ATTN_GROUPS):
        o, lse = _dilated_attention(qkv[3 * g], qkv[3 * g + 1], qkv[3 * g + 2], g, batch, seq)
        outs.append(o)
        lses.append(lse)
    h = _merge_out_proj(outs, lses, attn_w_out[0].astype(BF16), h, seq)
    h = _hierarchical_moe(h, norm_ffn[0], router_group_w[0], router_group_b[0],
                          router_expert_w[0], router_expert_b[0], w1, w3, w2, 0)

    u = _norm_proj(h, norm_mix[1], pool_w_in[0].astype(BF16))
    z = _pool_mixer_inner(u, pool_w_group[0].astype(BF16), pool_scale[0], batch, seq)
    h = _proj_residual(z, pool_w_out[0].astype(BF16), h)
    out = _hierarchical_moe(h, norm_ffn[1], router_group_w[1], router_group_b[1],
                            router_expert_w[1], router_expert_b[1], w1, w3, w2, 1,
                            final_gain=norm_final)
    return out.reshape(batch, seq, D_MODEL)
```

```python
import functools
import math

import jax
import jax.numpy as jnp
from jax import lax
from jax.experimental import pallas as pl
from jax.experimental.pallas import tpu as pltpu

F32 = jnp.float32
BF16 = jnp.bfloat16

D_MODEL = 1024
ATTN_PATTERNS = ((128, 1), (512, 4), (2048, 16))
N_ATTN_GROUPS = len(ATTN_PATTERNS)
HEADS = 8
HEAD_DIM = 64
ATTN_WIDTH = HEADS * HEAD_DIM
QKV_WIDTH = N_ATTN_GROUPS * 3 * ATTN_WIDTH
ROT_DIM = HEAD_DIM // 4
ROPE_THETA = 500000.0
ATTN_BLOCK = 128
POOL_WINDOWS = (2, 4, 8, 16)
N_POOL_GROUPS = len(POOL_WINDOWS)
POOL_GROUP_DIM = D_MODEL // N_POOL_GROUPS
N_EXPERT_GROUPS = 4
EXPERTS_PER_GROUP = 4
N_EXPERTS = N_EXPERT_GROUPS * EXPERTS_PER_GROUP
D_EXPERT = D_MODEL // 2
RMS_EPS = 1e-6

PAIRS = ((0, 1), (0, 2), (0, 3), (1, 2), (1, 3), (2, 3))
N_CLASSES = N_EXPERT_GROUPS * len(PAIRS)

ROW_TILE = 512
COL_TILE = 512
MOE_ROWS = 256
LANES = 128
VMEM_LIMIT = 48 * 1024 * 1024
MASKED = -1e30


def _params(sem):
    return pltpu.CompilerParams(dimension_semantics=sem, vmem_limit_bytes=VMEM_LIMIT)


def _rmsnorm_rows(x, g):
    ms = jnp.mean(x * x, axis=-1, keepdims=True)
    return (x * lax.rsqrt(ms + RMS_EPS)) * g


def _rope_kernel(pos_ref, freq_ref, c_ref, s1_ref, s2_ref):
    ang = pos_ref[...].astype(F32) * freq_ref[...]
    lane = lax.broadcasted_iota(jnp.int32, ang.shape, 1) % HEAD_DIM
    sin = jnp.sin(ang)
    c_ref[...] = jnp.cos(ang)
    s1_ref[...] = jnp.where(lane < ROT_DIM // 2, -sin, 0.0)
    s2_ref[...] = jnp.where(lane >= ROT_DIM // 2, sin, 0.0)


def _rope_tables(positions):
    t = positions.size
    half = ROT_DIM // 2
    inv = [ROPE_THETA ** (-(i * 2.0 / ROT_DIM)) for i in range(half)]
    per_head = inv + inv + [0.0] * (HEAD_DIM - ROT_DIM)
    freq = jnp.asarray([per_head * (LANES // HEAD_DIM)], F32)
    tm = 1024
    out = jax.ShapeDtypeStruct((t, LANES), F32)
    spec = pl.BlockSpec((tm, LANES), lambda i: (i, 0))
    return pl.pallas_call(
        _rope_kernel,
        grid=(t // tm,),
        in_specs=[pl.BlockSpec((tm, 1), lambda i: (i, 0)),
                  pl.BlockSpec((1, LANES), lambda i: (0, 0))],
        out_specs=[spec, spec, spec],
        out_shape=[out, out, out],
        compiler_params=_params(("parallel",)),
        name="rope_tables",
    )(positions.reshape(t, 1), freq)


def _qkv_kernel(x_ref, g_ref, w_ref, c_ref, s1_ref, s2_ref, *rest):
    o_refs, (y_sc, acc_sc) = rest[:-2], rest[-2:]
    y_sc[...] = _rmsnorm_rows(x_ref[...], g_ref[...]).astype(BF16)
    chunks = COL_TILE // LANES
    for col in range(QKV_WIDTH // COL_TILE):
        dil = ATTN_PATTERNS[col // 3][1]
        n = ROW_TILE // dil
        buf = col % 2
        acc = jnp.dot(y_sc[...], w_ref[:, col * COL_TILE:(col + 1) * COL_TILE],
                      preferred_element_type=F32)
        for c in range(chunks):
            acc_sc[buf, c] = acc[:, c * LANES:(c + 1) * LANES]
        rotary = col % 3 != 2
        for p in range(dil):
            rows = pl.ds(p, n, stride=dil) if dil > 1 else slice(None)
            if rotary:
                cos, sin_up, sin_dn = c_ref[rows, :], s1_ref[rows, :], s2_ref[rows, :]
            for c in range(chunks):
                a = acc_sc[buf, c, rows, :]
                if rotary:
                    a = (a * cos + pltpu.roll(a, LANES - ROT_DIM // 2, 1) * sin_up
                         + pltpu.roll(a, ROT_DIM // 2, 1) * sin_dn)
                if col % 3 == 0:
                    a = a * (1.0 / math.sqrt(HEAD_DIM))
                lanes = slice(p * COL_TILE + c * LANES, p * COL_TILE + (c + 1) * LANES)
                o_refs[col][:, lanes] = a.astype(BF16)


def _qkv_proj(x, gain, w, c, s1, s2, batch, seq):
    t = x.shape[0]
    tiles = seq // ROW_TILE
    tab = pl.BlockSpec((ROW_TILE, LANES), lambda i: (i, 0))
    out_specs, out_shapes = [], []
    for _, dil in ATTN_PATTERNS:
        for _ in range(3):
            out_specs.append(pl.BlockSpec((None, ROW_TILE // dil, dil * ATTN_WIDTH),
                                          lambda i: (i // tiles, i % tiles, 0)))
            out_shapes.append(jax.ShapeDtypeStruct((batch, seq // dil, dil * ATTN_WIDTH), BF16))
    return pl.pallas_call(
        _qkv_kernel,
        grid=(t // ROW_TILE,),
        in_specs=[pl.BlockSpec((ROW_TILE, D_MODEL), lambda i: (i, 0)),
                  pl.BlockSpec((1, D_MODEL), lambda i: (0, 0)),
                  pl.BlockSpec((D_MODEL, QKV_WIDTH), lambda i: (0, 0)),
                  tab, tab, tab],
        out_specs=out_specs,
        out_shape=out_shapes,
        scratch_shapes=[pltpu.VMEM((ROW_TILE, D_MODEL), BF16),
                        pltpu.VMEM((2, COL_TILE // LANES, ROW_TILE, LANES), F32)],
        compiler_params=_params(("parallel",)),
        name="qkv_proj",
    )(x, gain.reshape(1, D_MODEL), w, c, s1, s2)


def _attn_kernel(q_ref, kp_ref, kc_ref, vp_ref, vc_ref, o_ref, l_ref):
    j = pl.program_id(2)
    blk = ATTN_BLOCK
    row = lax.broadcasted_iota(jnp.int32, (blk, 2 * blk), 0)
    col = lax.broadcasted_iota(jnp.int32, (blk, 2 * blk), 1)
    band = jnp.where(col >= row, jnp.where(col <= row + blk, 0.0, MASKED), MASKED)
    lo = jnp.maximum(row, jnp.where(j == 0, blk, 0))
    band_first = jnp.where(col >= lo, band, MASKED)
    first = lax.broadcasted_iota(jnp.int32, (blk, LANES), 1) < HEAD_DIM
    keep = (jnp.where(first, 1.0, 0.0).astype(BF16), jnp.where(first, 0.0, 1.0).astype(BF16))
    for sub in range(q_ref.shape[0] // blk):
        rows = slice(sub * blk, (sub + 1) * blk)
        before = slice((sub - 1) * blk, sub * blk)
        bias = band_first if sub == 0 else band
        for pair in range(q_ref.shape[1] // LANES):
            sl = slice(pair * LANES, (pair + 1) * LANES)
            q2 = q_ref[rows, sl]
            k_prev = kp_ref[:, sl] if sub == 0 else kc_ref[before, sl]
            v_prev = vp_ref[:, sl] if sub == 0 else vc_ref[before, sl]
            k2 = jnp.concatenate([k_prev, kc_ref[rows, sl]], axis=0)
            v2 = jnp.concatenate([v_prev, vc_ref[rows, sl]], axis=0)
            o_half, r_half, l_half = [], [], []
            for half in range(2):
                s = lax.dot_general(q2 * keep[half], k2, (((1,), (1,)), ((), ())),
                                    preferred_element_type=F32) + bias
                m = jnp.max(s, axis=-1, keepdims=True)
                p = jnp.exp(s - m)
                den = jnp.sum(p, axis=-1, keepdims=True)
                o_half.append(jnp.dot(p.astype(BF16), v2, preferred_element_type=F32))
                r_half.append(1.0 / den)
                l_half.append(m + jnp.log(den))
            o = jnp.where(first, o_half[0], o_half[1]) * jnp.where(first, r_half[0], r_half[1])
            o_ref[rows, sl] = o.astype(BF16)
            l_ref[rows, sl] = jnp.where(first, l_half[0], l_half[1])


def _dilated_attention(q, k, v, group, batch, seq):
    _, dil = ATTN_PATTERNS[group]
    length = seq // dil
    nb = length // ATTN_BLOCK
    units = 4
    phases = min(dil, units)
    blocks = units // phases
    width = phases * ATTN_WIDTH
    cur = pl.BlockSpec((None, blocks * ATTN_BLOCK, width), lambda b, p, j: (b, j, p))
    prev = pl.BlockSpec((None, ATTN_BLOCK, width),
                        lambda b, p, j: (b, jnp.maximum(blocks * j - 1, 0), p))
    return pl.pallas_call(
        _attn_kernel,
        grid=(batch, dil // phases, nb // blocks),
        in_specs=[cur, prev, cur, prev, cur],
        out_specs=[cur, cur],
        out_shape=[jax.ShapeDtypeStruct((batch, length, dil * ATTN_WIDTH), BF16),
                   jax.ShapeDtypeStruct((batch, length, dil * ATTN_WIDTH), F32)],
        compiler_params=_params(("parallel", "parallel", "arbitrary")),
        name=f"dilated_attn_g{group}",
    )(q, k, k, v, v)


def _natural_rows(ref, stage, dil):
    if dil == 1:
        return ref[...].astype(F32)
    n = ROW_TILE // dil
    chunks = ATTN_WIDTH // LANES
    for p in range(dil):
        for c in range(chunks):
            lanes = slice(p * ATTN_WIDTH + c * LANES, p * ATTN_WIDTH + (c + 1) * LANES)
            stage[c, pl.ds(p, n, stride=dil), :] = ref[:, lanes].astype(F32)
    return jnp.concatenate([stage[c] for c in range(chunks)], axis=1)


def _merge_kernel(o0, o1, o2, l0, l1, l2, w_ref, h_ref, out_ref, so1, so2, sl1, sl2):
    dils = [d for _, d in ATTN_PATTERNS]
    a = _natural_rows(l0, None, dils[0])
    b = _natural_rows(l1, sl1, dils[1])
    c = _natural_rows(l2, sl2, dils[2])
    m = jnp.maximum(jnp.maximum(a, b), c)
    ea, eb, ec = jnp.exp(a - m), jnp.exp(b - m), jnp.exp(c - m)
    mix = ea * _natural_rows(o0, None, dils[0])
    mix += eb * _natural_rows(o1, so1, dils[1])
    mix += ec * _natural_rows(o2, so2, dils[2])
    mix = mix / (ea + eb + ec)
    out_ref[...] = h_ref[...] + jnp.dot(mix.astype(BF16), w_ref[...], preferred_element_type=F32)


def _merge_out_proj(outs, lses, w, h, seq):
    t = h.shape[0]
    tiles = seq // ROW_TILE
    slabs = [pl.BlockSpec((None, ROW_TILE // d, d * ATTN_WIDTH), lambda i: (i // tiles, i % tiles, 0))
             for _, d in ATTN_PATTERNS]
    full = pl.BlockSpec((ROW_TILE, D_MODEL), lambda i: (i, 0))
    stage = pltpu.VMEM((ATTN_WIDTH // LANES, ROW_TILE, LANES), F32)
    return pl.pallas_call(
        _merge_kernel,
        grid=(t // ROW_TILE,),
        in_specs=slabs + slabs + [pl.BlockSpec((ATTN_WIDTH, D_MODEL), lambda i: (0, 0)), full],
        out_specs=full,
        out_shape=jax.ShapeDtypeStruct((t, D_MODEL), F32),
        scratch_shapes=[stage, stage, stage, stage],
        compiler_params=_params(("parallel",)),
        name="attn_merge_out",
    )(*outs, *lses, w, h)


CLASS_ROWS = 32


def _router_kernel(x_ref, g_ref, whi_ref, wlo_ref, b_ref, tri_ref,
                   cls_ref, rank_ref, gate_ref, count_ref, run_sc):
    @pl.when(pl.program_id(0) == 0)
    def _():
        run_sc[...] = jnp.zeros_like(run_sc)

    y = _rmsnorm_rows(x_ref[...], g_ref[...])
    y_hi = y.astype(BF16)
    y_lo = (y - y_hi.astype(F32)).astype(BF16)
    nt = (((1,), (1,)), ((), ()))
    whi = whi_ref[...]
    lg = lax.dot_general(whi, y_hi, nt, preferred_element_type=F32)
    lg += lax.dot_general(whi, y_lo, nt, preferred_element_type=F32)
    lg += lax.dot_general(wlo_ref[...], y_hi, nt, preferred_element_type=F32)
    lg = lg + b_ref[...]
    row = [lg[r:r + 1, :] for r in range(N_EXPERT_GROUPS + N_EXPERTS)]

    best, grp = row[0], jnp.zeros(row[0].shape, jnp.int32)
    for g in range(1, N_EXPERT_GROUPS):
        better = row[g] > best
        grp = jnp.where(better, g, grp)
        best = jnp.where(better, row[g], best)
    gate1 = 1.0 / sum(jnp.exp(row[g] - best) for g in range(N_EXPERT_GROUPS))

    le = []
    for e in range(EXPERTS_PER_GROUP):
        v = row[N_EXPERT_GROUPS + e]
        for g in range(1, N_EXPERT_GROUPS):
            v = jnp.where(grp == g, row[N_EXPERT_GROUPS + g * EXPERTS_PER_GROUP + e], v)
        le.append(v)
    v1, i1 = le[0], jnp.zeros(grp.shape, jnp.int32)
    for e in range(1, EXPERTS_PER_GROUP):
        better = le[e] > v1
        i1 = jnp.where(better, e, i1)
        v1 = jnp.where(better, le[e], v1)
    v2, i2 = jnp.full(v1.shape, -jnp.inf, F32), jnp.zeros(grp.shape, jnp.int32)
    for e in range(EXPERTS_PER_GROUP):
        cand = jnp.where(i1 == e, -jnp.inf, le[e])
        better = cand > v2
        i2 = jnp.where(better, e, i2)
        v2 = jnp.where(better, cand, v2)
    ratio = jnp.exp(v2 - v1)
    g_first = gate1 / (1.0 + ratio)
    g_second = gate1 * ratio / (1.0 + ratio)
    first_low = i1 < i2
    lo = jnp.where(first_low, i1, i2)
    hi = jnp.where(first_low, i2, i1)
    pair = jnp.right_shift(lo * (2 * EXPERTS_PER_GROUP - 1 - lo), 1) + (hi - lo - 1)
    cls = grp * len(PAIRS) + pair
    cls_ref[...] = cls
    gate_ref[0:1, :] = jnp.where(first_low, g_first, g_second)
    gate_ref[1:2, :] = jnp.where(first_low, g_second, g_first)

    onehot = lax.broadcasted_iota(jnp.int32, (CLASS_ROWS, ROW_TILE), 0) == cls
    oh = jnp.where(onehot, 1.0, 0.0)
    earlier = jnp.dot(oh.astype(BF16), tri_ref[...], preferred_element_type=F32)
    running = run_sc[:, 0:1]
    rank_ref[...] = jnp.sum(oh * (earlier + running), axis=0, keepdims=True).astype(jnp.int32)
    run_sc[...] = run_sc[...] + jnp.sum(oh, axis=1, keepdims=True)
    count_ref[...] = run_sc[...]


def _route_tokens(h, gain, wcat, bias):
    t = h.shape[0]
    wt = wcat.T
    whi = wt.astype(BF16)
    wlo = (wt - whi.astype(F32)).astype(BF16)
    idx = jnp.arange(ROW_TILE, dtype=jnp.int32)
    tri = (idx[:, None] < idx[None, :]).astype(BF16)
    wspec = pl.BlockSpec((LANES, D_MODEL), lambda i: (0, 0))
    row_i = pl.BlockSpec((1, ROW_TILE), lambda i: (0, i))
    return pl.pallas_call(
        _router_kernel,
        grid=(t // ROW_TILE,),
        in_specs=[pl.BlockSpec((ROW_TILE, D_MODEL), lambda i: (i, 0)),
                  pl.BlockSpec((1, D_MODEL), lambda i: (0, 0)), wspec, wspec,
                  pl.BlockSpec((LANES, 1), lambda i: (0, 0)),
                  pl.BlockSpec((ROW_TILE, ROW_TILE), lambda i: (0, 0))],
        out_specs=[row_i, row_i, pl.BlockSpec((2, ROW_TILE), lambda i: (0, i)),
                   pl.BlockSpec((CLASS_ROWS, LANES), lambda i: (0, 0))],
        out_shape=[jax.ShapeDtypeStruct((1, t), jnp.int32),
                   jax.ShapeDtypeStruct((1, t), jnp.int32),
                   jax.ShapeDtypeStruct((2, t), F32),
                   jax.ShapeDtypeStruct((CLASS_ROWS, LANES), F32)],
        scratch_shapes=[pltpu.VMEM((CLASS_ROWS, LANES), F32)],
        compiler_params=_params(("arbitrary",)),
        name="router_top2",
    )(h, gain.reshape(1, D_MODEL), whi, wlo, bias, tri)


ROW_PAYLOAD = D_MODEL + LANES
DMA_GROUP = 8


def _dispatch_kernel(dest_ref, h_ref, g_ref, gate_ref, xs_init, xs_hbm, rowbuf, sem):
    del xs_init
    base = pl.program_id(0) * ROW_TILE
    rowbuf[:, :D_MODEL] = _rmsnorm_rows(h_ref[...], g_ref[...])
    rowbuf[:, D_MODEL:] = gate_ref[...]

    def row_copy(r):
        return pltpu.make_async_copy(rowbuf.at[pl.ds(r, 1), :],
                                     xs_hbm.at[pl.ds(dest_ref[base + r], 1), :], sem)

    def start(g, carry):
        for k in range(DMA_GROUP):
            row_copy(g * DMA_GROUP + k).start(priority=k % 2)
        return carry

    def wait(r, carry):
        row_copy(r).wait()
        return carry

    lax.fori_loop(0, ROW_TILE // DMA_GROUP, start, 0)
    lax.fori_loop(0, ROW_TILE, wait, 0, unroll=8)


def _dispatch_rows(h, gain, gate_pad, dest, rows):
    t = h.shape[0]
    grid_spec = pltpu.PrefetchScalarGridSpec(
        num_scalar_prefetch=1,
        grid=(t // ROW_TILE,),
        in_specs=[pl.BlockSpec((ROW_TILE, D_MODEL), lambda i, d: (i, 0)),
                  pl.BlockSpec((1, D_MODEL), lambda i, d: (0, 0)),
                  pl.BlockSpec((ROW_TILE, LANES), lambda i, d: (i, 0)),
                  pl.BlockSpec(memory_space=pl.ANY)],
        out_specs=pl.BlockSpec(memory_space=pl.ANY),
        scratch_shapes=[pltpu.VMEM((ROW_TILE, ROW_PAYLOAD), F32), pltpu.SemaphoreType.DMA(())],
    )
    return pl.pallas_call(
        _dispatch_kernel,
        grid_spec=grid_spec,
        out_shape=jax.ShapeDtypeStruct((rows, ROW_PAYLOAD), F32),
        input_output_aliases={4: 0},
        compiler_params=_params(("arbitrary",)),
        name="moe_dispatch",
    )(dest, h, gain.reshape(1, D_MODEL), gate_pad, jnp.zeros((rows, ROW_PAYLOAD), F32))


def _combine_kernel(dest_ref, h_ref, g_ref, ys_hbm, o_ref, buf, sem, *, final_norm):
    base = pl.program_id(0) * ROW_TILE

    def row_copy(r):
        return pltpu.make_async_copy(ys_hbm.at[pl.ds(dest_ref[base + r], 1), :],
                                     buf.at[pl.ds(r, 1), :], sem)

    def start(g, carry):
        for k in range(DMA_GROUP):
            row_copy(g * DMA_GROUP + k).start(priority=k % 2)
        return carry

    def wait(r, carry):
        row_copy(r).wait()
        return carry

    lax.fori_loop(0, ROW_TILE // DMA_GROUP, start, 0)
    lax.fori_loop(0, ROW_TILE, wait, 0, unroll=8)
    out = h_ref[...] + buf[...]
    o_ref[...] = _rmsnorm_rows(out, g_ref[...]) if final_norm else out


def _combine_rows(h, ys, dest, final_gain):
    t = h.shape[0]
    final_norm = final_gain is not None
    gain = final_gain if final_norm else jnp.ones((D_MODEL,), F32)
    tile = pl.BlockSpec((ROW_TILE, D_MODEL), lambda i, d: (i, 0))
    grid_spec = pltpu.PrefetchScalarGridSpec(
        num_scalar_prefetch=1,
        grid=(t // ROW_TILE,),
        in_specs=[tile, pl.BlockSpec((1, D_MODEL), lambda i, d: (0, 0)),
                  pl.BlockSpec(memory_space=pl.ANY)],
        out_specs=tile,
        scratch_shapes=[pltpu.VMEM((ROW_TILE, D_MODEL), F32), pltpu.SemaphoreType.DMA(())],
    )
    return pl.pallas_call(
        functools.partial(_combine_kernel, final_norm=final_norm),
        grid_spec=grid_spec,
        out_shape=jax.ShapeDtypeStruct((t, D_MODEL), F32),
        compiler_params=_params(("arbitrary",)),
        name="moe_combine",
    )(dest, h, gain.reshape(1, D_MODEL), ys)


def _silu(a):
    return a / (1.0 + jnp.exp(-a))


def _moe_kernel(ea_ref, eb_ref, x_ref, w1a, w3a, w2a, w1b, w3b, w2b, o_ref):
    del ea_ref, eb_ref
    x = x_ref[:, :D_MODEL].astype(BF16)
    g = x_ref[:, D_MODEL:]
    ha = _silu(jnp.dot(x, w1a[...], preferred_element_type=F32))
    ha = ha * jnp.dot(x, w3a[...], preferred_element_type=F32) * g[:, 0:1]
    hb = _silu(jnp.dot(x, w1b[...], preferred_element_type=F32))
    hb = hb * jnp.dot(x, w3b[...], preferred_element_type=F32) * g[:, 1:2]
    out = jnp.dot(ha.astype(BF16), w2a[...], preferred_element_type=F32)
    out += jnp.dot(hb.astype(BF16), w2b[...], preferred_element_type=F32)
    o_ref[...] = out


def _expert_blocks(xs, block_ea, block_eb, w1, w3, w2, layer):
    rows = xs.shape[0]
    n_blocks = rows // MOE_ROWS
    up = (None, None, D_MODEL, D_EXPERT)
    down = (None, None, D_EXPERT, D_MODEL)
    grid_spec = pltpu.PrefetchScalarGridSpec(
        num_scalar_prefetch=2,
        grid=(n_blocks,),
        in_specs=[pl.BlockSpec((MOE_ROWS, ROW_PAYLOAD), lambda i, ea, eb: (i, 0)),
                  pl.BlockSpec(up, lambda i, ea, eb: (layer, ea[i], 0, 0)),
                  pl.BlockSpec(up, lambda i, ea, eb: (layer, ea[i], 0, 0)),
                  pl.BlockSpec(down, lambda i, ea, eb: (layer, ea[i], 0, 0)),
                  pl.BlockSpec(up, lambda i, ea, eb: (layer, eb[i], 0, 0)),
                  pl.BlockSpec(up, lambda i, ea, eb: (layer, eb[i], 0, 0)),
                  pl.BlockSpec(down, lambda i, ea, eb: (layer, eb[i], 0, 0))],
        out_specs=pl.BlockSpec((MOE_ROWS, D_MODEL), lambda i, ea, eb: (i, 0)),
    )
    return pl.pallas_call(
        _moe_kernel,
        grid_spec=grid_spec,
        out_shape=jax.ShapeDtypeStruct((rows, D_MODEL), F32),
        compiler_params=_params(("arbitrary",)),
        name="expert_pair_blocks",
    )(block_ea, block_eb, xs, w1, w3, w2, w1, w3, w2)


def _hierarchical_moe(h, gain, wg, bg, we, be, w1, w3, w2, layer, final_gain=None):
    t = h.shape[0]
    pad = LANES - N_EXPERT_GROUPS - N_EXPERTS
    wcat = jnp.concatenate(
        [wg, we.transpose(1, 0, 2).reshape(D_MODEL, N_EXPERTS), jnp.zeros((D_MODEL, pad), F32)], axis=1)
    bias = jnp.concatenate([bg.astype(F32), be.astype(F32).reshape(N_EXPERTS), jnp.zeros((pad,), F32)])
    cls, rank, gates, counts = _route_tokens(h, gain, wcat, bias.reshape(LANES, 1))

    counts = counts[:N_CLASSES, 0].astype(jnp.int32)
    padded = ((counts + MOE_ROWS - 1) // MOE_ROWS) * MOE_ROWS
    pends = jnp.cumsum(padded)
    pstarts = pends - padded
    classes = jnp.arange(N_CLASSES, dtype=jnp.int32)
    cls = cls.reshape(t)
    dest = rank.reshape(t) + jnp.sum(jnp.where(cls[:, None] == classes[None, :], pstarts[None, :], 0), axis=1)
    n_blocks = -(-(t + N_CLASSES * (MOE_ROWS - 1)) // MOE_ROWS)
    rows = n_blocks * MOE_ROWS
    block_start = jnp.arange(n_blocks, dtype=jnp.int32) * MOE_ROWS
    block_cls = jnp.minimum(jnp.sum((block_start[:, None] >= pends[None, :]).astype(jnp.int32), axis=1),
                            N_CLASSES - 1)
    pair_lo = jnp.asarray([p[0] for p in PAIRS], jnp.int32)
    pair_hi = jnp.asarray([p[1] for p in PAIRS], jnp.int32)
    grp = block_cls // len(PAIRS)
    block_ea = grp * EXPERTS_PER_GROUP + pair_lo[block_cls % len(PAIRS)]
    block_eb = grp * EXPERTS_PER_GROUP + pair_hi[block_cls % len(PAIRS)]

    gate_pad = jnp.pad(gates.T, ((0, 0), (0, LANES - gates.shape[0])))
    xs = _dispatch_rows(h, gain, gate_pad, dest, rows)
    ys = _expert_blocks(xs, block_ea, block_eb, w1, w3, w2, layer)
    return _combine_rows(h, ys, dest, final_gain)


def _norm_proj_kernel(x_ref, g_ref, w_ref, o_ref):
    y = _rmsnorm_rows(x_ref[...], g_ref[...]).astype(BF16)
    o_ref[...] = jnp.dot(y, w_ref[...], preferred_element_type=F32)


def _norm_proj(x, gain, w):
    t, n = x.shape[0], w.shape[1]
    return pl.pallas_call(
        _norm_proj_kernel,
        grid=(t // ROW_TILE,),
        in_specs=[pl.BlockSpec((ROW_TILE, D_MODEL), lambda i: (i, 0)),
                  pl.BlockSpec((1, D_MODEL), lambda i: (0, 0)),
                  pl.BlockSpec((D_MODEL, n), lambda i: (0, 0))],
        out_specs=pl.BlockSpec((ROW_TILE, n), lambda i: (i, 0)),
        out_shape=jax.ShapeDtypeStruct((t, n), F32),
        compiler_params=_params(("parallel",)),
        name="norm_proj",
    )(x, gain.reshape(1, D_MODEL), w)


def _proj_residual_kernel(z_ref, w_ref, h_ref, o_ref):
    o_ref[...] = h_ref[...] + jnp.dot(z_ref[...], w_ref[...], preferred_element_type=F32)


def _proj_residual(z, w, h):
    t, n = h.shape
    k = z.shape[1]
    return pl.pallas_call(
        _proj_residual_kernel,
        grid=(t // ROW_TILE,),
        in_specs=[pl.BlockSpec((ROW_TILE, k), lambda i: (i, 0)),
                  pl.BlockSpec((k, n), lambda i: (0, 0)),
                  pl.BlockSpec((ROW_TILE, n), lambda i: (i, 0))],
        out_specs=pl.BlockSpec((ROW_TILE, n), lambda i: (i, 0)),
        out_shape=jax.ShapeDtypeStruct((t, n), F32),
        compiler_params=_params(("parallel",)),
        name="proj_residual",
    )(z, w, h)


def _pool_kernel(u_ref, wg_ref, sc_ref, z_ref):
    seq = u_ref.shape[0]
    row = lax.broadcasted_iota(jnp.int32, (seq, POOL_GROUP_DIM), 0)
    for g, window in enumerate(POOL_WINDOWS):
        cols = slice(g * POOL_GROUP_DIM, (g + 1) * POOL_GROUP_DIM)
        u = u_ref[:, cols]
        s = u
        step = 1
        while step < window:
            s = s + jnp.where(row >= step, pltpu.roll(s, step, 0), 0.0)
            step *= 2
        count = jnp.minimum(row + 1, window).astype(F32)
        pooled = (s / count - u).astype(BF16)
        z = jnp.dot(pooled, wg_ref[g], preferred_element_type=F32) * sc_ref[:, cols]
        z_ref[:, cols] = z.astype(BF16)


def _pool_mixer_inner(u, w_group, scale, batch, seq):
    blk = pl.BlockSpec((None, seq, D_MODEL), lambda b: (b, 0, 0))
    z = pl.pallas_call(
        _pool_kernel,
        grid=(batch,),
        in_specs=[blk,
                  pl.BlockSpec((N_POOL_GROUPS, POOL_GROUP_DIM, POOL_GROUP_DIM), lambda b: (0, 0, 0)),
                  pl.BlockSpec((1, D_MODEL), lambda b: (0, 0))],
        out_specs=blk,
        out_shape=jax.ShapeDtypeStruct((batch, seq, D_MODEL), BF16),
        compiler_params=_params(("parallel",)),
        name="causal_pool",
    )(u.reshape(batch, seq, D_MODEL), w_group, scale.reshape(1, D_MODEL))
    return z.reshape(batch * seq, D_MODEL)


def kernel(x, positions, norm_mix, norm_ffn, norm_final, attn_w_in, attn_w_out, pool_w_in, pool_w_group, pool_scale, pool_w_out, router_group_w, router_group_b, router_expert_w, router_expert_b, expert_w1, expert_w3, expert_w2):
    batch, seq, _ = x.shape
    t = batch * seq
    h = x.reshape(t, D_MODEL)
    w1 = expert_w1.astype(BF16)
    w3 = expert_w3.astype(BF16)
    w2 = expert_w2.astype(BF16)

    c, s1, s2 = _rope_tables(positions)
    qkv = _qkv_proj(h, norm_mix[0], attn_w_in[0].astype(BF16), c, s1, s2, batch, seq)
    outs, lses = [], []
    for g in range(N_ATTN_GROUPS):
        o, lse = _dilated_attention(qkv[3 * g], qkv[3 * g + 1], qkv[3 * g + 2], g, batch, seq)
        outs.append(o)
        lses.append(lse)
    h = _merge_out_proj(outs, lses, attn_w_out[0].astype(BF16), h, seq)
    h = _hierarchical_moe(h, norm_ffn[0], router_group_w[0], router_group_b[0],
                          router_expert_w[0], router_expert_b[0], w1, w3, w2, 0)

    u = _norm_proj(h, norm_mix[1], pool_w_in[0].astype(BF16))
    z = _pool_mixer_inner(u, pool_w_group[0].astype(BF16), pool_scale[0], batch, seq)
    h = _proj_residual(z, pool_w_out[0].astype(BF16), h)
    out = _hierarchical_moe(h, norm_ffn[1], router_group_w[1], router_group_b[1],
                            router_expert_w[1], router_expert_b[1], w1, w3, w2, 1,
                            final_gain=norm_final)
    return out.reshape(batch, seq, D_MODEL)
```
